```python
import math
import jax, jax.numpy as jnp
from jax import lax
import numpy as np

D_MODEL = 4096
BATCH = 1
SEQ = 16384
DEPTH = 4

N_MIXERS = 2
D_FF = 6144
NORM_EPS = 1e-6
N_NORMS = 6
POOL_WINDOWS = (2, 4, 8, 16)
N_POOL_GROUPS = len(POOL_WINDOWS)
POOL_GROUP_WIDTH = D_MODEL // N_POOL_GROUPS
N_HEADS = 32
HEAD_DIM = D_MODEL // N_HEADS
N_KV_GROUPS = 4
HEADS_PER_GROUP = N_HEADS // N_KV_GROUPS
N_BRANCHES = 3
CMP_BLOCK = 32
CMP_STRIDE = 16
SEL_BLOCK = 64
N_SELECT = 16
WINDOW = 512
Q_BLOCK = 128
QKV_WIDTH = N_HEADS * HEAD_DIM + 2 * N_BRANCHES * N_KV_GROUPS * HEAD_DIM + N_BRANCHES * N_HEADS
N_BUCKETS = 32
MAX_DISTANCE = 4096
FORCE_SCORE = 1e6
NEG_INF = -1e30

kernel_name = "hybrid_pool_nsa_macaron_sandwich"


def rms_norm(x, g):
    xf = x.astype(jnp.float32)
    y = xf * lax.rsqrt(jnp.mean(xf * xf, axis=-1, keepdims=True) + NORM_EPS)
    return (y * g.astype(jnp.float32)).astype(x.dtype)


def swiglu(h, w_gate, w_up, w_down):
    return (jax.nn.silu(h @ w_gate) * (h @ w_up)) @ w_down


def masked_softmax(logits, mask):
    logits = jnp.where(mask, logits.astype(jnp.float32), NEG_INF)
    m = jnp.max(logits, axis=-1, keepdims=True)
    e = jnp.where(mask, jnp.exp(logits - m), 0.0)
    return e / jnp.maximum(jnp.sum(e, axis=-1, keepdims=True), 1e-30)


def rel_bucket(dist):
    dist = jnp.maximum(dist, 0)
    max_exact = N_BUCKETS // 2
    d_f = jnp.maximum(dist, 1).astype(jnp.float32)
    large = max_exact + (jnp.log(d_f / max_exact) / math.log(MAX_DISTANCE / max_exact)
                         * (N_BUCKETS - max_exact)).astype(jnp.int32)
    large = jnp.minimum(large, N_BUCKETS - 1)
    return jnp.where(dist < max_exact, dist, large)


def pool_mixer(h, w_in, w_group, scale, w_out):
    B, S, _ = h.shape
    u = (h @ w_in).astype(jnp.float32).reshape(B, S, N_POOL_GROUPS, POOL_GROUP_WIDTH)
    cs = jnp.pad(jnp.cumsum(u, axis=1), ((0, 0), (1, 0), (0, 0), (0, 0)))
    outs = []
    for g, w in enumerate(POOL_WINDOWS):
        upper = cs[:, 1:, g]
        lower = jnp.pad(cs[:, :, g], ((0, 0), (w - 1, 0), (0, 0)))[:, :S]
        count = jnp.minimum(jnp.arange(S) + 1, w).astype(jnp.float32)[None, :, None]
        outs.append((upper - lower) / count - u[:, :, g])
    d = jnp.stack(outs, axis=2)
    y = jnp.einsum('bsgc,gcd->bsgd', d, w_group.astype(jnp.float32))
    y = (y * scale.astype(jnp.float32).reshape(N_POOL_GROUPS, POOL_GROUP_WIDTH)).reshape(B, S, D_MODEL)
    return y.astype(h.dtype) @ w_out


def compress_blocks(k, pos_emb, w1, w2):
    B, S, G, dh = k.shape
    c = CMP_BLOCK // CMP_STRIDE
    n_chunks = S // CMP_STRIDE
    n_cmp = n_chunks - c + 1
    chunks = k.reshape(B, n_chunks, CMP_STRIDE, G, dh)
    blocks = jnp.concatenate([chunks[:, m:m + n_cmp] for m in range(c)], axis=2)
    blocks = blocks + pos_emb[:, None, :]
    flat = jnp.moveaxis(blocks, 3, 2).reshape(B, n_cmp, G, CMP_BLOCK * dh)
    return jax.nn.gelu(flat @ w1) @ w2


def nsa_mixer(h, w_in, cmp_pos, cmp_w1, cmp_w2, w_out, rel_table):
    B, S, _ = h.shape
    G, HG, dh = N_KV_GROUPS, HEADS_PER_GROUP, HEAD_DIM
    proj = h @ w_in
    q_end = N_HEADS * dh
    kv_end = q_end + 2 * N_BRANCHES * G * dh
    q = proj[..., :q_end].reshape(B, S, G, HG, dh)
    kv = proj[..., q_end:kv_end].reshape(B, S, N_BRANCHES, 2, G, dh)
    gates = jax.nn.sigmoid(proj[..., kv_end:].astype(jnp.float32)).reshape(B, S, N_BRANCHES, G, HG)

    k_cmp = compress_blocks(kv[:, :, 0, 0], cmp_pos[0], cmp_w1[0], cmp_w2[0])
    v_cmp = compress_blocks(kv[:, :, 0, 1], cmp_pos[1], cmp_w1[1], cmp_w2[1])
    n_cmp = k_cmp.shape[1]
    cmp_end = jnp.arange(n_cmp) * CMP_STRIDE + CMP_BLOCK - 1

    n_sel = S // SEL_BLOCK
    n_top = min(N_SELECT, n_sel)
    k_sel_b = kv[:, :, 1, 0].reshape(B, n_sel, SEL_BLOCK, G, dh).transpose(0, 3, 1, 2, 4)
    v_sel_b = kv[:, :, 1, 1].reshape(B, n_sel, SEL_BLOCK, G, dh).transpose(0, 3, 1, 2, 4)
    r = SEL_BLOCK // CMP_STRIDE
    c = CMP_BLOCK // CMP_STRIDE

    pad_w = ((0, 0), (WINDOW, 0), (0, 0), (0, 0))
    k_win_p = jnp.pad(kv[:, :, 2, 0], pad_w)
    v_win_p = jnp.pad(kv[:, :, 2, 1], pad_w)

    scale = dh ** -0.5
    table_g = rel_table.reshape(N_BUCKETS, G, HG).transpose(1, 0, 2)
    sel_offsets = jnp.arange(SEL_BLOCK)
    win_offsets = jnp.arange(Q_BLOCK + WINDOW) - WINDOW
    blk_ids = jnp.arange(n_sel)
    gather_blocks = jax.vmap(jax.vmap(lambda kb, ib: kb[ib]))
    bias_per_group = jax.vmap(lambda tb, bk: tb[bk], in_axes=(0, 1), out_axes=1)

    def head_bias(dist):
        return rel_table[rel_bucket(dist)].reshape(*dist.shape, G, HG).transpose(2, 3, 0, 1)

    def query_block(qb):
        s0 = qb * Q_BLOCK
        t = s0 + jnp.arange(Q_BLOCK)
        q_b = lax.dynamic_slice_in_dim(q, s0, Q_BLOCK, axis=1)
        g_b = lax.dynamic_slice_in_dim(gates, s0, Q_BLOCK, axis=1)

        dist_c = t[:, None] - cmp_end[None, :]
        logit_c = jnp.einsum('bqghd,bngd->bghqn', q_b, k_cmp) * scale + head_bias(dist_c)
        p_c = masked_softmax(logit_c, dist_c >= 0)
        o_c = jnp.einsum('bghqn,bngd->bqghd', p_c, v_cmp.astype(jnp.float32))

        imp = jnp.pad(jnp.sum(p_c, axis=2), ((0, 0), (0, 0), (0, 0), (c - 1, r + c)))
        imp_sel = sum(imp[..., o:o + n_sel * r:r] for o in range(r + c - 1))
        cur = (t // SEL_BLOCK)[:, None]
        j = blk_ids[None, :]
        forced = (j == 0) | (j == cur) | (j == cur - 1)
        valid = j * SEL_BLOCK <= t[:, None]
        score = jnp.where(valid, imp_sel + jnp.where(forced, FORCE_SCORE, 0.0), NEG_INF)
        _, idx = lax.top_k(score, n_top)

        n_keys = n_top * SEL_BLOCK
        k_g = gather_blocks(k_sel_b, idx).reshape(B, G, Q_BLOCK, n_keys, dh)
        v_g = gather_blocks(v_sel_b, idx).reshape(B, G, Q_BLOCK, n_keys, dh)
        pos_g = (idx[..., None] * SEL_BLOCK + sel_offsets).reshape(B, G, Q_BLOCK, n_keys)
        dist_s = t[:, None] - pos_g
        bias_s = jnp.moveaxis(bias_per_group(table_g, rel_bucket(dist_s)), -1, 2)
        logit_s = jnp.einsum('bqghd,bgqkd->bghqk', q_b, k_g) * scale + bias_s
        p_s = masked_softmax(logit_s, (dist_s >= 0)[:, :, None])
        o_s = jnp.einsum('bghqk,bgqkd->bqghd', p_s, v_g.astype(jnp.float32))

        k_w = lax.dynamic_slice_in_dim(k_win_p, s0, Q_BLOCK + WINDOW, axis=1)
        v_w = lax.dynamic_slice_in_dim(v_win_p, s0, Q_BLOCK + WINDOW, axis=1)
        pos_w = s0 + win_offsets
        dist_w = t[:, None] - pos_w[None, :]
        mask_w = (dist_w >= 0) & (dist_w < WINDOW) & (pos_w >= 0)[None, :]
        logit_w = jnp.einsum('bqghd,bkgd->bghqk', q_b, k_w) * scale + head_bias(dist_w)
        p_w = masked_softmax(logit_w, mask_w)
        o_w = jnp.einsum('bghqk,bkgd->bqghd', p_w, v_w.astype(jnp.float32))

        o = (g_b[:, :, 0, :, :, None] * o_c + g_b[:, :, 1, :, :, None] * o_s
             + g_b[:, :, 2, :, :, None] * o_w)
        return o.astype(h.dtype)

    o = lax.map(query_block, jnp.arange(S // Q_BLOCK))
    o = jnp.moveaxis(o, 0, 1).reshape(B, S, N_HEADS * dh)
    return o @ w_out


def setup_inputs(seed: int = 0) -> dict:
    key = jax.random.key(seed)
    ks = jax.random.split(key, 16)
    n_pool = (DEPTH + 1) // 2
    n_nsa = DEPTH // 2
    D, F = D_MODEL, D_FF
    nrm = lambda k, shape, fan_in: jax.random.normal(k, shape, jnp.float32) * fan_in ** -0.5
    return {
        "x": jax.random.normal(ks[0], (BATCH, SEQ, D), jnp.float32),
        "norm_gains": 1.0 + 0.05 * jax.random.normal(ks[1], (DEPTH, N_NORMS, D), jnp.float32),
        "ffn_w_gate": nrm(ks[2], (DEPTH, 2, D, F), D),
        "ffn_w_up": nrm(ks[3], (DEPTH, 2, D, F), D),
        "ffn_w_down": nrm(ks[4], (DEPTH, 2, F, D), F),
        "pool_w_in": nrm(ks[5], (n_pool, D, D), D),
        "pool_w_group": nrm(ks[6], (n_pool, N_POOL_GROUPS, POOL_GROUP_WIDTH, POOL_GROUP_WIDTH), POOL_GROUP_WIDTH),
        "pool_scale": 1.0 + 0.1 * jax.random.normal(ks[7], (n_pool, D), jnp.float32),
        "pool_w_out": nrm(ks[8], (n_pool, D, D), D),
        "nsa_w_in": nrm(ks[9], (n_nsa, D, QKV_WIDTH), D),
        "nsa_cmp_pos": 0.1 * jax.random.normal(ks[10], (n_nsa, 2, CMP_BLOCK, HEAD_DIM), jnp.float32),
        "nsa_cmp_w1": nrm(ks[11], (n_nsa, 2, CMP_BLOCK * HEAD_DIM, HEAD_DIM), CMP_BLOCK * HEAD_DIM),
        "nsa_cmp_w2": nrm(ks[12], (n_nsa, 2, HEAD_DIM, HEAD_DIM), HEAD_DIM),
        "nsa_w_out": nrm(ks[13], (n_nsa, N_HEADS * HEAD_DIM, D), N_HEADS * HEAD_DIM),
        "rel_bias": 0.5 * jax.random.normal(ks[14], (N_BUCKETS, N_HEADS), jnp.float32),
    }


def reference(x, norm_gains, ffn_w_gate, ffn_w_up, ffn_w_down, pool_w_in, pool_w_group, pool_scale,
              pool_w_out, nsa_w_in, nsa_cmp_pos, nsa_cmp_w1, nsa_cmp_w2, nsa_w_out, rel_bias):
    for i in range(DEPTH):
        g = norm_gains[i]
        f1 = swiglu(rms_norm(x, g[0]), ffn_w_gate[i, 0], ffn_w_up[i, 0], ffn_w_down[i, 0])
        x = x + 0.5 * rms_norm(f1, g[1])
        h = rms_norm(x, g[2])
        li = i // N_MIXERS
        if i % N_MIXERS == 0:
            m = pool_mixer(h, pool_w_in[li], pool_w_group[li], pool_scale[li], pool_w_out[li])
        else:
            m = nsa_mixer(h, nsa_w_in[li], nsa_cmp_pos[li], nsa_cmp_w1[li], nsa_cmp_w2[li],
                          nsa_w_out[li], rel_bias)
        x = x + rms_norm(m, g[3])
        f2 = swiglu(rms_norm(x, g[4]), ffn_w_gate[i, 1], ffn_w_up[i, 1], ffn_w_down[i, 1])
        x = x + 0.5 * rms_norm(f2, g[5])
    return x
```

```python
import functools
import math

import jax
import jax.numpy as jnp
from jax import lax
from jax.experimental import pallas as pl
from jax.experimental.pallas import tpu as pltpu

NORM_EPS = 1e-6
POOL_WINDOWS = (2, 4, 8, 16)
HEAD_DIM = 128
N_KV_GROUPS = 4
N_BRANCHES = 3
CMP_BLOCK = 32
CMP_STRIDE = 16
SEL_BLOCK = 64
SEL_BLOCK_LOG2 = 6
N_SELECT = 16
WINDOW = 512
N_BUCKETS = 32
MAX_EXACT = N_BUCKETS // 2
FORCE_SCORE = 1e6
NEG_INF = -1e30
BELOW_NEG_INF = -3e38

V7X_LANES = 128
V7X_SUBLANES = 8
V7X_VMEM_LIMIT_BYTES = 56 * 1024 * 1024

Q_TILE = 128
KEY_TILE = 256
CMP_TILE = 128


def _compiler_params(semantics):
    return pltpu.CompilerParams(dimension_semantics=semantics, vmem_limit_bytes=V7X_VMEM_LIMIT_BYTES)


def _rms(xf, gain):
    ms = jnp.mean(xf * xf, axis=-1, keepdims=True)
    return xf * lax.rsqrt(ms + NORM_EPS) * gain


def _norm_body(x_ref, g_ref, h_ref):
    h_ref[...] = _rms(x_ref[...], g_ref[...]).astype(h_ref.dtype)


def rms_norm_cast(x, gain, *, rows=256):
    s, d = x.shape
    return pl.pallas_call(
        _norm_body,
        grid=(s // rows,),
        in_specs=[pl.BlockSpec((rows, d), lambda i: (i, 0)), pl.BlockSpec((1, d), lambda i: (0, 0))],
        out_specs=pl.BlockSpec((rows, d), lambda i: (i, 0)),
        out_shape=jax.ShapeDtypeStruct((s, d), jnp.bfloat16),
        compiler_params=_compiler_params(("arbitrary",)),
        name="rms_norm_cast",
    )(x, gain.reshape(1, d))


def _resnorm_body(x_ref, f_ref, gp_ref, gn_ref, xo_ref, h_ref, *, alpha):
    y = x_ref[...] + alpha * _rms(f_ref[...], gp_ref[...])
    xo_ref[...] = y
    h_ref[...] = _rms(y, gn_ref[...]).astype(h_ref.dtype)


def _resnorm_last_body(x_ref, f_ref, gp_ref, xo_ref, *, alpha):
    xo_ref[...] = x_ref[...] + alpha * _rms(f_ref[...], gp_ref[...])


def residual_norm(x, f, g_post, g_next, alpha, *, rows=256):
    s, d = x.shape
    row_spec = pl.BlockSpec((rows, d), lambda i: (i, 0))
    gain_spec = pl.BlockSpec((1, d), lambda i: (0, 0))
    if g_next is None:
        return pl.pallas_call(
            functools.partial(_resnorm_last_body, alpha=alpha),
            grid=(s // rows,),
            in_specs=[row_spec, row_spec, gain_spec],
            out_specs=row_spec,
            out_shape=jax.ShapeDtypeStruct((s, d), jnp.float32),
            compiler_params=_compiler_params(("arbitrary",)),
            name="residual_norm_last",
        )(x, f, g_post.reshape(1, d)), None
    return pl.pallas_call(
        functools.partial(_resnorm_body, alpha=alpha),
        grid=(s // rows,),
        in_specs=[row_spec, row_spec, gain_spec, gain_spec],
        out_specs=[row_spec, row_spec],
        out_shape=[jax.ShapeDtypeStruct((s, d), jnp.float32), jax.ShapeDtypeStruct((s, d), jnp.bfloat16)],
        compiler_params=_compiler_params(("arbitrary",)),
        name="residual_norm",
    )(x, f, g_post.reshape(1, d), g_next.reshape(1, d))


def _ffn_body(h_ref, wg_ref, wu_ref, wd_ref, o_ref, a_ref, *, n_up, tf):
    j = pl.program_id(1)

    @pl.when(j < n_up)
    def _():
        h = h_ref[...]
        g = jnp.dot(h, wg_ref[...], preferred_element_type=jnp.float32)
        u = jnp.dot(h, wu_ref[...], preferred_element_type=jnp.float32)
        a = (g * jax.nn.sigmoid(g)) * u
        a_ref[:, pl.ds(pl.multiple_of(j * tf, tf), tf)] = a.astype(a_ref.dtype)

    @pl.when(j >= n_up)
    def _():
        o_ref[...] = jnp.dot(a_ref[...], wd_ref[...], preferred_element_type=jnp.float32)


def swiglu_ffn(h, w_gate, w_up, w_down, *, tm=1024, tf=256, tn=256):
    s, d = h.shape
    f = w_gate.shape[1]
    tm = min(tm, s)
    n_up, n_down = f // tf, d // tn
    return pl.pallas_call(
        functools.partial(_ffn_body, n_up=n_up, tf=tf),
        grid=(s // tm, n_up + n_down),
        in_specs=[
            pl.BlockSpec((tm, d), lambda i, j: (i, 0)),
            pl.BlockSpec((d, tf), lambda i, j: (0, jnp.minimum(j, n_up - 1))),
            pl.BlockSpec((d, tf), lambda i, j: (0, jnp.minimum(j, n_up - 1))),
            pl.BlockSpec((f, tn), lambda i, j: (0, jnp.maximum(j - n_up, 0))),
        ],
        out_specs=pl.BlockSpec((tm, tn), lambda i, j: (i, jnp.maximum(j - n_up, 0))),
        out_shape=jax.ShapeDtypeStruct((s, d), jnp.float32),
        scratch_shapes=[pltpu.VMEM((tm, f), jnp.bfloat16)],
        compiler_params=_compiler_params(("arbitrary", "arbitrary")),
        name="swiglu_ffn",
    )(h, w_gate, w_up, w_down)


def _matmul_body(a_ref, b_ref, o_ref, *, sigmoid):
    r = jnp.dot(a_ref[...], b_ref[...], preferred_element_type=jnp.float32)
    if sigmoid:
        r = jax.nn.sigmoid(r)
    o_ref[...] = r.astype(o_ref.dtype)


def matmul(a, b, out_dtype, *, tm=1024, tn=512, sigmoid=False, name="matmul"):
    m, k = a.shape
    n = b.shape[1]
    tm, tn = min(tm, m), min(tn, n)
    return pl.pallas_call(
        functools.partial(_matmul_body, sigmoid=sigmoid),
        grid=(m // tm, n // tn),
        in_specs=[pl.BlockSpec((tm, k), lambda i, j: (i, 0)), pl.BlockSpec((k, tn), lambda i, j: (0, j))],
        out_specs=pl.BlockSpec((tm, tn), lambda i, j: (i, j)),
        out_shape=jax.ShapeDtypeStruct((m, n), out_dtype),
        compiler_params=_compiler_params(("arbitrary", "arbitrary")),
        name=name,
    )(a, b)


POOL_HALO = 16


def _pool_body(h_ref, win_ref, wgrp_ref, scale_ref, y_ref, halo_ref, *, tm):
    grp = pl.program_id(0)
    i = pl.program_id(1)
    u = jnp.dot(h_ref[...], win_ref[...], preferred_element_type=jnp.float32)

    @pl.when(i == 0)
    def _():
        halo_ref[...] = jnp.zeros_like(halo_ref)

    ext = jnp.concatenate([halo_ref[...], u], axis=0)
    halo_ref[...] = u[tm - POOL_HALO:, :]
    s2 = ext + pltpu.roll(ext, 1, 0)
    s4 = s2 + pltpu.roll(s2, 2, 0)
    s8 = s4 + pltpu.roll(s4, 4, 0)
    s16 = s8 + pltpu.roll(s8, 8, 0)
    sw = jnp.where(grp == 0, s2, jnp.where(grp == 1, s4, jnp.where(grp == 2, s8, s16)))[POOL_HALO:, :]
    w = jnp.where(grp == 0, POOL_WINDOWS[0], jnp.where(grp == 1, POOL_WINDOWS[1],
                  jnp.where(grp == 2, POOL_WINDOWS[2], POOL_WINDOWS[3])))
    t = i * tm + lax.broadcasted_iota(jnp.int32, (tm, 1), 0)
    count = jnp.minimum(t + 1, w).astype(jnp.float32)
    d = sw / count - u
    y = jnp.dot(d.astype(jnp.bfloat16), wgrp_ref[0], preferred_element_type=jnp.float32)
    y_ref[...] = (y * scale_ref[...]).astype(y_ref.dtype)


def pool_front(h, w_in, w_group, scale, *, tm=512):
    s, d = h.shape
    n_groups, gw, _ = w_group.shape
    tm = min(tm, s)
    return pl.pallas_call(
        functools.partial(_pool_body, tm=tm),
        grid=(n_groups, s // tm),
        in_specs=[
            pl.BlockSpec((tm, d), lambda g, i: (i, 0)),
            pl.BlockSpec((d, gw), lambda g, i: (0, g)),
            pl.BlockSpec((1, gw, gw), lambda g, i: (g, 0, 0)),
            pl.BlockSpec((1, gw), lambda g, i: (0, g)),
        ],
        out_specs=pl.BlockSpec((tm, gw), lambda g, i: (i, g)),
        out_shape=jax.ShapeDtypeStruct((s, d), jnp.bfloat16),
        scratch_shapes=[pltpu.VMEM((POOL_HALO, gw), jnp.float32)],
        compiler_params=_compiler_params(("arbitrary", "arbitrary")),
        name="pool_front",
    )(h, w_in, w_group, scale.reshape(1, d))


def _bucket_thresholds():
    thr = list(range(1, MAX_EXACT + 1))
    for k in range(1, N_BUCKETS - MAX_EXACT):
        thr.append(math.isqrt(256 * 2 ** k - 1) + 1)
    return tuple(thr)


BUCKET_THRESHOLDS = _bucket_thresholds()
LAST_BUCKET_DIST = BUCKET_THRESHOLDS[-1]


def _strip_body(tab_ref, o_ref, *, rows, a_offset, lane_stride, dist_offset, max_dist):
    h = pl.program_id(0)
    r = pl.program_id(1)
    a = r * rows + lax.broadcasted_iota(jnp.int32, (rows, V7X_LANES), 0)
    l = lax.broadcasted_iota(jnp.int32, (rows, V7X_LANES), 1)
    dist = a - a_offset - lane_stride * l - dist_offset
    v = jnp.full((rows, V7X_LANES), tab_ref[0, h], jnp.float32)
    for b, thr in enumerate(BUCKET_THRESHOLDS, start=1):
        v = jnp.where(dist >= thr, tab_ref[b, h], v)
    ok = dist >= 0
    if max_dist is not None:
        ok = jnp.logical_and(ok, dist < max_dist)
    o_ref[0] = jnp.where(ok, v, NEG_INF)


STRIP_ROW_BLOCK = 512


def bias_strip(rel_table, min_rows, *, a_offset, lane_stride, dist_offset, max_dist):
    n_heads = rel_table.shape[1]
    rows = STRIP_ROW_BLOCK
    n_rows = -(-min_rows // rows) * rows
    return pl.pallas_call(
        functools.partial(_strip_body, rows=rows, a_offset=a_offset, lane_stride=lane_stride,
                          dist_offset=dist_offset, max_dist=max_dist),
        grid=(n_heads, n_rows // rows),
        in_specs=[pl.BlockSpec(memory_space=pltpu.SMEM)],
        out_specs=pl.BlockSpec((1, rows, V7X_LANES), lambda h, r: (h, r, 0)),
        out_shape=jax.ShapeDtypeStruct((n_heads, n_rows, V7X_LANES), jnp.float32),
        compiler_params=_compiler_params(("arbitrary", "arbitrary")),
        name="bias_strip",
    )(rel_table)


SEL_STRIP_OFFSET = KEY_TILE // 2
SEL_STRIP_CLAMP = LAST_BUCKET_DIST + V7X_LANES + SEL_STRIP_OFFSET
SEL_STRIP_CLAMP += (-SEL_STRIP_CLAMP) % V7X_SUBLANES
WIN_STRIP_CLAMP = WINDOW + V7X_LANES + SEL_STRIP_OFFSET
CMP_DIST_OFFSET = CMP_BLOCK - 1
CMP_STRIP_CLAMP = LAST_BUCKET_DIST + CMP_STRIDE * (V7X_LANES - 1) + CMP_DIST_OFFSET
CMP_STRIP_CLAMP += (-CMP_STRIP_CLAMP) % V7X_SUBLANES


def _bias_tile(strip_ref, h, row0, *, n_rows, lane_tiles, lane_step, clamp, lo=None):
    out_rows = []
    for i0 in range(0, n_rows, V7X_SUBLANES):
        pieces = []
        for c in range(lane_tiles):
            e = row0 + (i0 - lane_step * c)
            e = jnp.minimum(e, clamp)
            if lo is not None:
                e = jnp.maximum(e, lo)
            e = pl.multiple_of(e, V7X_SUBLANES)
            pieces.append(strip_ref[h, pl.ds(e, V7X_SUBLANES), :])
        out_rows.append(jnp.concatenate(pieces, axis=1) if lane_tiles > 1 else pieces[0])
    return jnp.concatenate(out_rows, axis=0)


def _compress_body(x_ref, pos_ref, w1a_ref, w1b_ref, w2_ref, o_ref, acca_ref, accb_ref, *, n_chunks):
    l = pl.program_id(1)

    @pl.when(l == 0)
    def _():
        acca_ref[...] = jnp.zeros_like(acca_ref)
        accb_ref[...] = jnp.zeros_like(accb_ref)

    x = x_ref[...]
    pa = pos_ref[0, pl.ds(l, 1), :]
    pb = pos_ref[0, pl.ds(l + CMP_STRIDE, 1), :]
    acca_ref[...] += jnp.dot((x + pa).astype(jnp.bfloat16), w1a_ref[0], preferred_element_type=jnp.float32)
    accb_ref[...] += jnp.dot((x + pb).astype(jnp.bfloat16), w1b_ref[0], preferred_element_type=jnp.float32)

    @pl.when(l == CMP_STRIDE - 1)
    def _():
        pre = acca_ref[...] + pltpu.roll(accb_ref[...], n_chunks - 1, 0)
        act = jax.nn.gelu(pre)
        o_ref[0] = jnp.dot(act.astype(jnp.bfloat16), w2_ref[0],
                           preferred_element_type=jnp.float32).astype(o_ref.dtype)


def compress_kv(kv_cmp, pos, w1, w2):
    s = kv_cmp.shape[0]
    dh = HEAD_DIM
    n_chunks = s // CMP_STRIDE
    n_kvg = 2 * N_KV_GROUPS
    x = kv_cmp.reshape(n_chunks, CMP_STRIDE * n_kvg * dh)
    w1r = w1.reshape(2, CMP_BLOCK, dh, dh)
    return pl.pallas_call(
        functools.partial(_compress_body, n_chunks=n_chunks),
        grid=(n_kvg, CMP_STRIDE),
        in_specs=[
            pl.BlockSpec((n_chunks, dh), lambda c, l: (0, l * n_kvg + c)),
            pl.BlockSpec((1, CMP_BLOCK, dh), lambda c, l: (c // N_KV_GROUPS, 0, 0)),
            pl.BlockSpec((1, None, dh, dh), lambda c, l: (c // N_KV_GROUPS, l, 0, 0)),
            pl.BlockSpec((1, None, dh, dh), lambda c, l: (c // N_KV_GROUPS, l + CMP_STRIDE, 0, 0)),
            pl.BlockSpec((1, dh, dh), lambda c, l: (c // N_KV_GROUPS, 0, 0)),
        ],
        out_specs=pl.BlockSpec((1, n_chunks, dh), lambda c, l: (c, 0, 0)),
        out_shape=jax.ShapeDtypeStruct((n_kvg, n_chunks, dh), jnp.bfloat16),
        scratch_shapes=[pltpu.VMEM((n_chunks, dh), jnp.float32), pltpu.VMEM((n_chunks, dh), jnp.float32)],
        compiler_params=_compiler_params(("arbitrary", "arbitrary")),
        name="compress_kv",
    )(x, pos, w1r, w1r, w2)


def _gate_column(gates, col):
    lane = lax.broadcasted_iota(jnp.int32, gates.shape, 1)
    return jnp.sum(jnp.where(lane == col, gates, 0.0), axis=-1, keepdims=True)


def _cmp_attn_body(q_ref, kc_ref, vc_ref, strip_ref, gates_ref, pool_ref, oc_ref, sel_ref, *,
                   n_cmp_pad, n_sel, heads_per_group, scale):
    g = pl.program_id(0)
    qb = pl.program_id(1)
    t0 = qb * Q_TILE
    dh = HEAD_DIM
    kc = kc_ref[0]
    vc = vc_ref[0]
    gates = gates_ref[...]
    lane_tiles = n_cmp_pad // CMP_TILE
    imp = jnp.zeros((Q_TILE, n_cmp_pad), jnp.float32)
    for h in range(heads_per_group):
        qh = q_ref[:, h * dh:(h + 1) * dh]
        s = lax.dot_general(qh, kc, (((1,), (1,)), ((), ())), preferred_element_type=jnp.float32) * scale
        s = s + _bias_tile(strip_ref, h, t0, n_rows=Q_TILE, lane_tiles=lane_tiles,
                           lane_step=CMP_STRIDE * CMP_TILE, clamp=CMP_STRIP_CLAMP, lo=0)
        valid = s > 0.5 * NEG_INF
        m = jnp.max(s, axis=-1, keepdims=True)
        e = jnp.where(valid, jnp.exp(s - m), 0.0)
        p = e / jnp.maximum(jnp.sum(e, axis=-1, keepdims=True), 1e-30)
        imp = imp + p
        o = jnp.dot(p.astype(jnp.bfloat16), vc, preferred_element_type=jnp.float32)
        gate = _gate_column(gates, g * heads_per_group + h)
        oc_ref[:, h * dh:(h + 1) * dh] = o * gate

    pool = pool_ref[...]
    hi = imp.astype(jnp.bfloat16)
    r1 = imp - hi.astype(jnp.float32)
    mid = r1.astype(jnp.bfloat16)
    lo = (r1 - mid.astype(jnp.float32)).astype(jnp.bfloat16)
    imp_sel = (jnp.dot(hi, pool, preferred_element_type=jnp.float32)
               + jnp.dot(mid, pool, preferred_element_type=jnp.float32)
               + jnp.dot(lo, pool, preferred_element_type=jnp.float32))

    j = lax.broadcasted_iota(jnp.int32, (Q_TILE, n_sel), 1)
    t = t0 + lax.broadcasted_iota(jnp.int32, (Q_TILE, n_sel), 0)
    cur = jnp.right_shift(t, SEL_BLOCK_LOG2)
    forced = jnp.logical_or(j == 0, jnp.logical_or(j == cur, j == cur - 1))
    valid = j * SEL_BLOCK <= t
    score = jnp.where(valid, imp_sel + jnp.where(forced, FORCE_SCORE, 0.0), NEG_INF)
    picked = jnp.zeros((Q_TILE, n_sel), jnp.float32)
    jf = j.astype(jnp.float32)
    for _ in range(min(N_SELECT, n_sel)):
        best = jnp.max(score, axis=-1, keepdims=True)
        first = jnp.min(jnp.where(score == best, jf, float(n_sel)), axis=-1, keepdims=True)
        hit = jf == first
        picked = jnp.where(hit, 1.0, picked)
        score = jnp.where(hit, BELOW_NEG_INF, score)
    sel_ref[0] = picked.astype(sel_ref.dtype)


def _pool_matrix(n_cmp_pad, n_sel):
    r = SEL_BLOCK // CMP_STRIDE
    c = CMP_BLOCK // CMP_STRIDE
    m = jnp.arange(n_cmp_pad)[:, None]
    j = jnp.arange(n_sel)[None, :]
    lo = r * j - (c - 1)
    return jnp.logical_and(m >= lo, m <= lo + r + c - 2).astype(jnp.bfloat16)


def cmp_attention(q, kv_c, strip, gates, *, scale):
    s, dq = q.shape
    n_heads = dq // HEAD_DIM
    hg = n_heads // N_KV_GROUPS
    n_cmp_pad = kv_c.shape[1]
    n_sel = s // SEL_BLOCK
    gw = hg * HEAD_DIM
    return pl.pallas_call(
        functools.partial(_cmp_attn_body, n_cmp_pad=n_cmp_pad, n_sel=n_sel, heads_per_group=hg, scale=scale),
        grid=(N_KV_GROUPS, s // Q_TILE),
        in_specs=[
            pl.BlockSpec((Q_TILE, gw), lambda g, qb: (qb, g)),
            pl.BlockSpec((1, n_cmp_pad, HEAD_DIM), lambda g, qb: (g, 0, 0)),
            pl.BlockSpec((1, n_cmp_pad, HEAD_DIM), lambda g, qb: (N_KV_GROUPS + g, 0, 0)),
            pl.BlockSpec((hg, strip.shape[1], V7X_LANES), lambda g, qb: (g, 0, 0), pipeline_mode=pl.Buffered(1)),
            pl.BlockSpec((Q_TILE, V7X_LANES), lambda g, qb: (qb, 0)),
            pl.BlockSpec((n_cmp_pad, n_sel), lambda g, qb: (0, 0)),
        ],
        out_specs=[
            pl.BlockSpec((Q_TILE, gw), lambda g, qb: (qb, g)),
            pl.BlockSpec((1, Q_TILE, n_sel), lambda g, qb: (g, qb, 0)),
        ],
        out_shape=[jax.ShapeDtypeStruct((s, dq), jnp.float32),
                   jax.ShapeDtypeStruct((N_KV_GROUPS, s, n_sel), jnp.bfloat16)],
        compiler_params=_compiler_params(("arbitrary", "arbitrary")),
        name="cmp_attention",
    )(q, kv_c, kv_c, strip, gates, _pool_matrix(n_cmp_pad, n_sel))


def _flash_step(q_ref, k, v, extra, strip_ref, row0, clamp, m_ref, l_ref, acc_ref, *, heads_per_group, scale):
    dh = HEAD_DIM
    lane_tiles = k.shape[0] // V7X_LANES
    for h in range(heads_per_group):
        qh = q_ref[:, h * dh:(h + 1) * dh]
        s = lax.dot_general(qh, k, (((1,), (1,)), ((), ())), preferred_element_type=jnp.float32) * scale
        b = _bias_tile(strip_ref, h, row0, n_rows=Q_TILE, lane_tiles=lane_tiles, lane_step=V7X_LANES, clamp=clamp)
        if extra is not None:
            b = b + extra
        s = s + b
        m_old = m_ref[h]
        m_new = jnp.maximum(m_old, jnp.max(s, axis=-1, keepdims=True))
        alpha = jnp.exp(m_old - m_new)
        p = jnp.exp(s - m_new)
        l_ref[h] = alpha * l_ref[h] + jnp.sum(p, axis=-1, keepdims=True)
        acc_ref[h] = alpha * acc_ref[h] + jnp.dot(p.astype(jnp.bfloat16), v, preferred_element_type=jnp.float32)
        m_ref[h] = m_new


def _reset_state(m_ref, l_ref, acc_ref):
    m_ref[...] = jnp.full_like(m_ref, NEG_INF)
    l_ref[...] = jnp.zeros_like(l_ref)
    acc_ref[...] = jnp.zeros_like(acc_ref)


def _sel_win_body(q_ref, ks_ref, vs_ref, kw0_ref, kw1_ref, kw2_ref, vw0_ref, vw1_ref, vw2_ref,
                  sel_ref, sstrip_ref, wstrip_ref, gates_ref, oc_ref, o_ref,
                  m_ref, l_ref, acc_ref, part_ref, *, heads_per_group, n_sel, scale):
    g = pl.program_id(0)
    qb = pl.program_id(1)
    t0 = qb * Q_TILE
    dh = HEAD_DIM
    hg = heads_per_group
    blocks_per_tile = KEY_TILE // SEL_BLOCK
    last_tile = (t0 + Q_TILE - 1) // KEY_TILE
    gates = gates_ref[...]

    _reset_state(m_ref, l_ref, acc_ref)
    sel = sel_ref[0]
    blk = lax.broadcasted_iota(jnp.int32, (n_sel, KEY_TILE), 0)
    key_blk = jnp.right_shift(lax.broadcasted_iota(jnp.int32, (n_sel, KEY_TILE), 1), SEL_BLOCK_LOG2)
    blk_minus_key = blk - key_blk

    def sel_step(kt, carry):
        start = pl.multiple_of(kt * KEY_TILE, KEY_TILE)
        k = ks_ref[pl.ds(start, KEY_TILE), :]
        v = vs_ref[pl.ds(start, KEY_TILE), :]
        expand = jnp.where(blk_minus_key == kt * blocks_per_tile, 1.0, 0.0).astype(jnp.bfloat16)
        chosen = jnp.dot(sel, expand, preferred_element_type=jnp.float32)
        extra = (chosen - 1.0) * (-NEG_INF)
        _flash_step(q_ref, k, v, extra, sstrip_ref, t0 - kt * KEY_TILE + SEL_STRIP_OFFSET, SEL_STRIP_CLAMP,
                    m_ref, l_ref, acc_ref, heads_per_group=hg, scale=scale)
        return carry

    lax.fori_loop(0, last_tile + 1, sel_step, 0)
    for h in range(hg):
        gate = _gate_column(gates, 4 * hg + g * hg + h)
        part_ref[:, h * dh:(h + 1) * dh] = oc_ref[:, h * dh:(h + 1) * dh] + acc_ref[h] / l_ref[h] * gate

    _reset_state(m_ref, l_ref, acc_ref)
    for back, (kw_ref, vw_ref) in enumerate(((kw0_ref, vw0_ref), (kw1_ref, vw1_ref), (kw2_ref, vw2_ref))):
        kt = last_tile - back

        @pl.when(kt >= 0)
        def _(kt=kt, kw_ref=kw_ref, vw_ref=vw_ref):
            _flash_step(q_ref, kw_ref[...], vw_ref[...], None, wstrip_ref,
                        t0 - kt * KEY_TILE + SEL_STRIP_OFFSET, WIN_STRIP_CLAMP,
                        m_ref, l_ref, acc_ref, heads_per_group=hg, scale=scale)

    for h in range(hg):
        gate = _gate_column(gates, 8 * hg + g * hg + h)
        o_ref[:, h * dh:(h + 1) * dh] = (part_ref[:, h * dh:(h + 1) * dh]
                                         + acc_ref[h] / l_ref[h] * gate).astype(o_ref.dtype)


def sel_win_attention(q, kv_sel, kv_win, sel, sel_strip, win_strip, gates, oc, *, scale):
    s, dq = q.shape
    n_heads = dq // HEAD_DIM
    hg = n_heads // N_KV_GROUPS
    gw = hg * HEAD_DIM
    n_sel = s // SEL_BLOCK
    once_per_group = pl.Buffered(1)

    def win_spec(back, is_v):
        def index(g, qb):
            kt = (qb * Q_TILE + Q_TILE - 1) // KEY_TILE - back
            return (jnp.maximum(kt, 0), g + (N_KV_GROUPS if is_v else 0))
        return pl.BlockSpec((KEY_TILE, HEAD_DIM), index)

    return pl.pallas_call(
        functools.partial(_sel_win_body, heads_per_group=hg, n_sel=n_sel, scale=scale),
        grid=(N_KV_GROUPS, s // Q_TILE),
        in_specs=[
            pl.BlockSpec((Q_TILE, gw), lambda g, qb: (qb, g)),
            pl.BlockSpec((s, HEAD_DIM), lambda g, qb: (0, g), pipeline_mode=once_per_group),
            pl.BlockSpec((s, HEAD_DIM), lambda g, qb: (0, N_KV_GROUPS + g), pipeline_mode=once_per_group),
            win_spec(0, False), win_spec(1, False), win_spec(2, False),
            win_spec(0, True), win_spec(1, True), win_spec(2, True),
            pl.BlockSpec((1, Q_TILE, n_sel), lambda g, qb: (g, qb, 0)),
            pl.BlockSpec((hg, sel_strip.shape[1], V7X_LANES), lambda g, qb: (g, 0, 0), pipeline_mode=once_per_group),
            pl.BlockSpec((hg, win_strip.shape[1], V7X_LANES), lambda g, qb: (g, 0, 0), pipeline_mode=once_per_group),
            pl.BlockSpec((Q_TILE, V7X_LANES), lambda g, qb: (qb, 0)),
            pl.BlockSpec((Q_TILE, gw), lambda g, qb: (qb, g)),
        ],
        out_specs=pl.BlockSpec((Q_TILE, gw), lambda g, qb: (qb, g)),
        out_shape=jax.ShapeDtypeStruct((s, dq), jnp.bfloat16),
        scratch_shapes=[pltpu.VMEM((hg, Q_TILE, 1), jnp.float32), pltpu.VMEM((hg, Q_TILE, 1), jnp.float32),
                        pltpu.VMEM((hg, Q_TILE, HEAD_DIM), jnp.float32), pltpu.VMEM((Q_TILE, gw), jnp.float32)],
        compiler_params=_compiler_params(("arbitrary", "arbitrary")),
        name="sel_win_attention",
    )(q, kv_sel, kv_sel, kv_win, kv_win, kv_win, kv_win, kv_win, kv_win, sel, sel_strip, win_strip, gates, oc)


def _bf16(w):
    return w.astype(jnp.bfloat16)


def nsa_strips(rel_bias):
    sel_strip = bias_strip(rel_bias, SEL_STRIP_CLAMP + V7X_SUBLANES, a_offset=SEL_STRIP_OFFSET, lane_stride=1,
                           dist_offset=0, max_dist=None)
    win_strip = bias_strip(rel_bias, WIN_STRIP_CLAMP + V7X_SUBLANES, a_offset=SEL_STRIP_OFFSET, lane_stride=1,
                           dist_offset=0, max_dist=WINDOW)
    cmp_strip = bias_strip(rel_bias, CMP_STRIP_CLAMP + V7X_SUBLANES, a_offset=0, lane_stride=CMP_STRIDE,
                           dist_offset=CMP_DIST_OFFSET, max_dist=None)
    return sel_strip, win_strip, cmp_strip


def nsa_mixer(h, w_in, cmp_pos, cmp_w1, cmp_w2, w_out, strips):
    s, d = h.shape
    dh, g = HEAD_DIM, N_KV_GROUPS
    q_end = w_out.shape[0]
    branch_w = 2 * g * dh
    kv_end = q_end + N_BRANCHES * branch_w
    n_gates = w_in.shape[1] - kv_end
    scale = dh ** -0.5
    sel_strip, win_strip, cmp_strip = strips

    q = matmul(h, _bf16(w_in[:, :q_end]), jnp.bfloat16, name="nsa_q_proj")
    kv_cmp = matmul(h, _bf16(w_in[:, q_end:q_end + branch_w]), jnp.float32, name="nsa_kv_cmp_proj")
    kv_sel = matmul(h, _bf16(w_in[:, q_end + branch_w:q_end + 2 * branch_w]), jnp.bfloat16, name="nsa_kv_sel_proj")
    kv_win = matmul(h, _bf16(w_in[:, q_end + 2 * branch_w:kv_end]), jnp.bfloat16, name="nsa_kv_win_proj")
    w_gate = jnp.pad(_bf16(w_in[:, kv_end:]), ((0, 0), (0, V7X_LANES - n_gates)))
    gates = matmul(h, w_gate, jnp.float32, sigmoid=True, name="nsa_gate_proj")

    kv_c = compress_kv(kv_cmp, cmp_pos, _bf16(cmp_w1), _bf16(cmp_w2))
    oc, sel = cmp_attention(q, kv_c, cmp_strip, gates, scale=scale)
    o = sel_win_attention(q, kv_sel, kv_win, sel, sel_strip, win_strip, gates, oc, scale=scale)
    return matmul(o, _bf16(w_out), jnp.float32, name="nsa_out_proj")


def pool_mixer(h, w_in, w_group, scale, w_out):
    y = pool_front(h, _bf16(w_in), _bf16(w_group), scale)
    return matmul(y, _bf16(w_out), jnp.float32, name="pool_out_proj")


def kernel(x, norm_gains, ffn_w_gate, ffn_w_up, ffn_w_down, pool_w_in, pool_w_group, pool_scale, pool_w_out,
           nsa_w_in, nsa_cmp_pos, nsa_cmp_w1, nsa_cmp_w2, nsa_w_out, rel_bias):
    b, s, d = x.shape
    depth = norm_gains.shape[0]
    outs = []
    strips = nsa_strips(rel_bias) if depth > 1 else None
    for bi in range(b):
        xs = x[bi]
        h = rms_norm_cast(xs, norm_gains[0, 0])
        for i in range(depth):
            gains = norm_gains[i]
            f1 = swiglu_ffn(h, _bf16(ffn_w_gate[i, 0]), _bf16(ffn_w_up[i, 0]), _bf16(ffn_w_down[i, 0]))
            xs, h = residual_norm(xs, f1, gains[1], gains[2], 0.5)
            li = i // 2
            if i % 2 == 0:
                m = pool_mixer(h, pool_w_in[li], pool_w_group[li], pool_scale[li], pool_w_out[li])
            else:
                m = nsa_mixer(h, nsa_w_in[li], nsa_cmp_pos[li], nsa_cmp_w1[li], nsa_cmp_w2[li], nsa_w_out[li],
                              strips)
            xs, h = residual_norm(xs, m, gains[3], gains[4], 1.0)
            f2 = swiglu_ffn(h, _bf16(ffn_w_gate[i, 1]), _bf16(ffn_w_up[i, 1]), _bf16(ffn_w_down[i, 1]))
            g_next = norm_gains[i + 1, 0] if i + 1 < depth else None
            xs, h = residual_norm(xs, f2, gains[5], g_next, 0.5)
        outs.append(xs)
    return jnp.stack(outs, axis=0)
```

```python
import functools
import math

import jax
import jax.numpy as jnp
from jax import lax
from jax.experimental import pallas as pl
from jax.experimental.pallas import tpu as pltpu

NORM_EPS = 1e-6
POOL_WINDOWS = (2, 4, 8, 16)
HEAD_DIM = 128
N_KV_GROUPS = 4
N_BRANCHES = 3
CMP_BLOCK = 32
CMP_STRIDE = 16
SEL_BLOCK = 64
SEL_BLOCK_LOG2 = 6
N_SELECT = 16
WINDOW = 512
N_BUCKETS = 32
MAX_EXACT = N_BUCKETS // 2
FORCE_SCORE = 1e6
NEG_INF = -1e30
BELOW_NEG_INF = -3e38
LOG2_E = math.log2(math.e)

V7X_LANES = 128
V7X_SUBLANES = 8
V7X_MXU_WIDTH = 256
V7X_VMEM_LIMIT_BYTES = 56 * 1024 * 1024

Q_TILE = 128
KEY_TILE = 256


def _compiler_params(semantics):
    return pltpu.CompilerParams(dimension_semantics=semantics, vmem_limit_bytes=V7X_VMEM_LIMIT_BYTES)


def _round_up(x, m):
    return -(-x // m) * m


def _rms(xf, gain):
    ms = jnp.mean(xf * xf, axis=-1, keepdims=True)
    return xf * lax.rsqrt(ms + NORM_EPS) * gain


_NT = (((1,), (1,)), ((), ()))


def _norm_body(x_ref, g_ref, h_ref):
    h_ref[...] = _rms(x_ref[...], g_ref[...]).astype(h_ref.dtype)


def rms_norm_cast(x, gain, *, rows=256):
    s, d = x.shape
    return pl.pallas_call(
        _norm_body,
        grid=(s // rows,),
        in_specs=[pl.BlockSpec((rows, d), lambda i: (i, 0)), pl.BlockSpec((1, d), lambda i: (0, 0))],
        out_specs=pl.BlockSpec((rows, d), lambda i: (i, 0)),
        out_shape=jax.ShapeDtypeStruct((s, d), jnp.bfloat16),
        compiler_params=_compiler_params(("arbitrary",)),
        name="rms_norm_cast",
    )(x, gain.reshape(1, d))


def _resnorm_body(x_ref, f_ref, gp_ref, gn_ref, xo_ref, h_ref, *, alpha):
    y = x_ref[...] + alpha * _rms(f_ref[...], gp_ref[...])
    xo_ref[...] = y
    h_ref[...] = _rms(y, gn_ref[...]).astype(h_ref.dtype)


def _resnorm_last_body(x_ref, f_ref, gp_ref, xo_ref, *, alpha):
    xo_ref[...] = x_ref[...] + alpha * _rms(f_ref[...], gp_ref[...])


def residual_norm(x, f, g_post, g_next, alpha, *, rows=256):
    s, d = x.shape
    row_spec = pl.BlockSpec((rows, d), lambda i: (i, 0))
    gain_spec = pl.BlockSpec((1, d), lambda i: (0, 0))
    if g_next is None:
        return pl.pallas_call(
            functools.partial(_resnorm_last_body, alpha=alpha),
            grid=(s // rows,),
            in_specs=[row_spec, row_spec, gain_spec],
            out_specs=row_spec,
            out_shape=jax.ShapeDtypeStruct((s, d), jnp.float32),
            compiler_params=_compiler_params(("arbitrary",)),
            name="residual_norm_last",
        )(x, f, g_post.reshape(1, d)), None
    return pl.pallas_call(
        functools.partial(_resnorm_body, alpha=alpha),
        grid=(s // rows,),
        in_specs=[row_spec, row_spec, gain_spec, gain_spec],
        out_specs=[row_spec, row_spec],
        out_shape=[jax.ShapeDtypeStruct((s, d), jnp.float32), jax.ShapeDtypeStruct((s, d), jnp.bfloat16)],
        compiler_params=_compiler_params(("arbitrary",)),
        name="residual_norm",
    )(x, f, g_post.reshape(1, d), g_next.reshape(1, d))


def _ffn_body(h_ref, wg_ref, wu_ref, wd_ref, o_ref, a_ref, *, n_up, tf):
    j = pl.program_id(1)

    @pl.when(j < n_up)
    def _():
        h = h_ref[...]
        g = jnp.dot(h, wg_ref[...], preferred_element_type=jnp.float32)
        u = jnp.dot(h, wu_ref[...], preferred_element_type=jnp.float32)
        a = (g * jax.nn.sigmoid(g)) * u
        a_ref[:, pl.ds(pl.multiple_of(j * tf, tf), tf)] = a.astype(a_ref.dtype)

    @pl.when(j >= n_up)
    def _():
        o_ref[...] = jnp.dot(a_ref[...], wd_ref[...], preferred_element_type=jnp.float32)


def swiglu_ffn(h, w_gate, w_up, w_down, *, tm=1024, tf=256, tn=256):
    s, d = h.shape
    f = w_gate.shape[1]
    tm = min(tm, s)
    n_up, n_down = f // tf, d // tn
    return pl.pallas_call(
        functools.partial(_ffn_body, n_up=n_up, tf=tf),
        grid=(s // tm, n_up + n_down),
        in_specs=[
            pl.BlockSpec((tm, d), lambda i, j: (i, 0)),
            pl.BlockSpec((d, tf), lambda i, j: (0, jnp.minimum(j, n_up - 1))),
            pl.BlockSpec((d, tf), lambda i, j: (0, jnp.minimum(j, n_up - 1))),
            pl.BlockSpec((f, tn), lambda i, j: (0, jnp.maximum(j - n_up, 0))),
        ],
        out_specs=pl.BlockSpec((tm, tn), lambda i, j: (i, jnp.maximum(j - n_up, 0))),
        out_shape=jax.ShapeDtypeStruct((s, d), jnp.float32),
        scratch_shapes=[pltpu.VMEM((tm, f), jnp.bfloat16)],
        compiler_params=_compiler_params(("arbitrary", "arbitrary")),
        name="swiglu_ffn",
    )(h, w_gate, w_up, w_down)


def _matmul_body(a_ref, b_ref, o_ref):
    o_ref[...] = jnp.dot(a_ref[...], b_ref[...], preferred_element_type=jnp.float32).astype(o_ref.dtype)


def matmul(a, b, out_dtype, *, tm=1024, tn=512, name="matmul"):
    m, k = a.shape
    n = b.shape[1]
    tm, tn = min(tm, m), min(tn, n)
    return pl.pallas_call(
        _matmul_body,
        grid=(m // tm, n // tn),
        in_specs=[pl.BlockSpec((tm, k), lambda i, j: (i, 0)), pl.BlockSpec((k, tn), lambda i, j: (0, j))],
        out_specs=pl.BlockSpec((tm, tn), lambda i, j: (i, j)),
        out_shape=jax.ShapeDtypeStruct((m, n), out_dtype),
        compiler_params=_compiler_params(("arbitrary", "arbitrary")),
        name=name,
    )(a, b)


def _matmul_nt_body(bt_ref, a_ref, o_ref, *, sigmoid):
    r = lax.dot_general(bt_ref[...], a_ref[...], _NT, preferred_element_type=jnp.float32)
    if sigmoid:
        r = jax.nn.sigmoid(r)
    o_ref[...] = r.astype(o_ref.dtype)


def matmul_nt(bt, a, out_dtype, *, tm=1024, sigmoid=False, name="matmul_nt"):
    n, k = bt.shape
    m = a.shape[0]
    tm = min(tm, m)
    return pl.pallas_call(
        functools.partial(_matmul_nt_body, sigmoid=sigmoid),
        grid=(m // tm,),
        in_specs=[pl.BlockSpec((n, k), lambda i: (0, 0)), pl.BlockSpec((tm, k), lambda i: (i, 0))],
        out_specs=pl.BlockSpec((n, tm), lambda i: (0, i)),
        out_shape=jax.ShapeDtypeStruct((n, m), out_dtype),
        compiler_params=_compiler_params(("arbitrary",)),
        name=name,
    )(bt, a)


POOL_HALO = 16


def _pool_body(h_ref, win_ref, wgrp_ref, scale_ref, y_ref, halo_ref, *, tm):
    grp = pl.program_id(0)
    i = pl.program_id(1)
    u = jnp.dot(h_ref[...], win_ref[...], preferred_element_type=jnp.float32)

    @pl.when(i == 0)
    def _():
        halo_ref[...] = jnp.zeros_like(halo_ref)

    ext = jnp.concatenate([halo_ref[...], u], axis=0)
    halo_ref[...] = u[tm - POOL_HALO:, :]
    s2 = ext + pltpu.roll(ext, 1, 0)
    s4 = s2 + pltpu.roll(s2, 2, 0)
    s8 = s4 + pltpu.roll(s4, 4, 0)
    s16 = s8 + pltpu.roll(s8, 8, 0)
    sw = jnp.where(grp == 0, s2, jnp.where(grp == 1, s4, jnp.where(grp == 2, s8, s16)))[POOL_HALO:, :]
    w = jnp.where(grp == 0, POOL_WINDOWS[0], jnp.where(grp == 1, POOL_WINDOWS[1],
                  jnp.where(grp == 2, POOL_WINDOWS[2], POOL_WINDOWS[3])))
    t = i * tm + lax.broadcasted_iota(jnp.int32, (tm, 1), 0)
    count = jnp.minimum(t + 1, w).astype(jnp.float32)
    d = sw / count - u
    y = jnp.dot(d.astype(jnp.bfloat16), wgrp_ref[0], preferred_element_type=jnp.float32)
    y_ref[...] = (y * scale_ref[...]).astype(y_ref.dtype)


def pool_front(h, w_in, w_group, scale, *, tm=512):
    s, d = h.shape
    n_groups, gw, _ = w_group.shape
    tm = min(tm, s)
    return pl.pallas_call(
        functools.partial(_pool_body, tm=tm),
        grid=(n_groups, s // tm),
        in_specs=[
            pl.BlockSpec((tm, d), lambda g, i: (i, 0)),
            pl.BlockSpec((d, gw), lambda g, i: (0, g)),
            pl.BlockSpec((1, gw, gw), lambda g, i: (g, 0, 0)),
            pl.BlockSpec((1, gw), lambda g, i: (0, g)),
        ],
        out_specs=pl.BlockSpec((tm, gw), lambda g, i: (i, g)),
        out_shape=jax.ShapeDtypeStruct((s, d), jnp.bfloat16),
        scratch_shapes=[pltpu.VMEM((POOL_HALO, gw), jnp.float32)],
        compiler_params=_compiler_params(("arbitrary", "arbitrary")),
        name="pool_front",
    )(h, w_in, w_group, scale.reshape(1, d))


def _bucket_thresholds():
    thr = list(range(1, MAX_EXACT + 1))
    for k in range(1, N_BUCKETS - MAX_EXACT):
        thr.append(math.isqrt(256 * 2 ** k - 1) + 1)
    return tuple(thr)


BUCKET_THRESHOLDS = _bucket_thresholds()
LAST_BUCKET_DIST = BUCKET_THRESHOLDS[-1]
STRIP_ROW_BLOCK = 256


def _strip_body(tab_ref, o_ref, *, rows, row_step, const, max_dist):
    h = pl.program_id(0)
    r = pl.program_id(1)
    a = r * rows + lax.broadcasted_iota(jnp.int32, (rows, V7X_LANES), 0)
    l = lax.broadcasted_iota(jnp.int32, (rows, V7X_LANES), 1)
    dist = l + const - row_step * a
    v = jnp.full((rows, V7X_LANES), tab_ref[0, h], jnp.float32)
    for b, thr in enumerate(BUCKET_THRESHOLDS, start=1):
        v = jnp.where(dist >= thr, tab_ref[b, h], v)
    ok = dist >= 0
    if max_dist is not None:
        ok = jnp.logical_and(ok, dist < max_dist)
    o_ref[0] = jnp.where(ok, v * LOG2_E, NEG_INF)


def bias_strip(rel_table, min_rows, *, row_step, const, max_dist):
    n_heads = rel_table.shape[1]
    rows = STRIP_ROW_BLOCK
    n_rows = _round_up(min_rows, rows)
    return pl.pallas_call(
        functools.partial(_strip_body, rows=rows, row_step=row_step, const=const, max_dist=max_dist),
        grid=(n_heads, n_rows // rows),
        in_specs=[pl.BlockSpec(memory_space=pltpu.SMEM)],
        out_specs=pl.BlockSpec((1, rows, V7X_LANES), lambda h, r: (h, r, 0)),
        out_shape=jax.ShapeDtypeStruct((n_heads, n_rows, V7X_LANES), jnp.float32),
        compiler_params=_compiler_params(("arbitrary", "arbitrary")),
        name="bias_strip",
    )(rel_table)


SEL_STRIP_CONST = _round_up(LAST_BUCKET_DIST + V7X_SUBLANES - 1, V7X_SUBLANES)
SEL_STRIP_ROWS = SEL_STRIP_CONST + KEY_TILE
WIN_STRIP_CONST = _round_up(WINDOW + V7X_SUBLANES - 1, V7X_SUBLANES)
WIN_STRIP_ROWS = WIN_STRIP_CONST + KEY_TILE
CMP_DIST_OFFSET = CMP_BLOCK - 1
CMP_STRIP_CONST = _round_up(LAST_BUCKET_DIST + CMP_STRIDE * (V7X_SUBLANES - 1) + CMP_DIST_OFFSET,
                            CMP_STRIDE * V7X_SUBLANES) - CMP_DIST_OFFSET
CMP_STRIP_ROW0 = (CMP_STRIP_CONST + CMP_DIST_OFFSET) // CMP_STRIDE
CMP_STRIP_HI = _round_up((V7X_LANES + CMP_STRIP_CONST) // CMP_STRIDE + 1, V7X_SUBLANES)
CMP_STRIP_ROWS = CMP_STRIP_HI + V7X_SUBLANES


def nsa_strips(rel_bias):
    sel_strip = bias_strip(rel_bias, SEL_STRIP_ROWS, row_step=1, const=SEL_STRIP_CONST, max_dist=None)
    win_strip = bias_strip(rel_bias, WIN_STRIP_ROWS, row_step=1, const=WIN_STRIP_CONST, max_dist=WINDOW)
    cmp_strip = bias_strip(rel_bias, CMP_STRIP_ROWS, row_step=CMP_STRIDE, const=CMP_STRIP_CONST, max_dist=None)
    return sel_strip, win_strip, cmp_strip


def _bias_columns(strip_ref, heads, offsets):
    cols = [jnp.concatenate([strip_ref[h, pl.ds(e, V7X_SUBLANES), :] for e in offsets], axis=0) for h in heads]
    return jnp.concatenate(cols, axis=1) if len(cols) > 1 else cols[0]


def _compress_body(x_ref, pos_ref, w1a_ref, w1b_ref, w2_ref, o_ref, acca_ref, accb_ref, *, n_chunks):
    l = pl.program_id(1)

    @pl.when(l == 0)
    def _():
        acca_ref[...] = jnp.zeros_like(acca_ref)
        accb_ref[...] = jnp.zeros_like(accb_ref)

    x = x_ref[...]
    pa = pos_ref[0, pl.ds(l, 1), :]
    pb = pos_ref[0, pl.ds(l + CMP_STRIDE, 1), :]
    acca_ref[...] += jnp.dot((x + pa).astype(jnp.bfloat16), w1a_ref[0], preferred_element_type=jnp.float32)
    accb_ref[...] += jnp.dot((x + pb).astype(jnp.bfloat16), w1b_ref[0], preferred_element_type=jnp.float32)

    @pl.when(l == CMP_STRIDE - 1)
    def _():
        pre = acca_ref[...] + pltpu.roll(accb_ref[...], n_chunks - 1, 0)
        act = jax.nn.gelu(pre)
        o_ref[0] = jnp.dot(act.astype(jnp.bfloat16), w2_ref[0],
                           preferred_element_type=jnp.float32).astype(o_ref.dtype)


def compress_kv(kv_cmp, pos, w1, w2):
    s = kv_cmp.shape[0]
    dh = HEAD_DIM
    n_chunks = s // CMP_STRIDE
    n_kvg = 2 * N_KV_GROUPS
    x = kv_cmp.reshape(n_chunks, CMP_STRIDE * n_kvg * dh)
    w1r = w1.reshape(2, CMP_BLOCK, dh, dh)
    return pl.pallas_call(
        functools.partial(_compress_body, n_chunks=n_chunks),
        grid=(n_kvg, CMP_STRIDE),
        in_specs=[
            pl.BlockSpec((n_chunks, dh), lambda c, l: (0, l * n_kvg + c)),
            pl.BlockSpec((1, CMP_BLOCK, dh), lambda c, l: (c // N_KV_GROUPS, 0, 0)),
            pl.BlockSpec((1, None, dh, dh), lambda c, l: (c // N_KV_GROUPS, l, 0, 0)),
            pl.BlockSpec((1, None, dh, dh), lambda c, l: (c // N_KV_GROUPS, l + CMP_STRIDE, 0, 0)),
            pl.BlockSpec((1, dh, dh), lambda c, l: (c // N_KV_GROUPS, 0, 0)),
        ],
        out_specs=pl.BlockSpec((1, n_chunks, dh), lambda c, l: (c, 0, 0)),
        out_shape=jax.ShapeDtypeStruct((n_kvg, n_chunks, dh), jnp.bfloat16),
        scratch_shapes=[pltpu.VMEM((n_chunks, dh), jnp.float32), pltpu.VMEM((n_chunks, dh), jnp.float32)],
        compiler_params=_compiler_params(("arbitrary", "arbitrary")),
        name="compress_kv",
    )(x, pos, w1r, w1r, w2)


def _online_softmax_tile(s, vt, m, l, acc_ref):
    m_new = jnp.maximum(m, jnp.max(s, axis=0, keepdims=True))
    alpha = jnp.exp2(m - m_new)
    p = jnp.exp2(s - m_new)
    l_new = alpha * l + jnp.sum(p, axis=0, keepdims=True)
    acc_ref[...] = alpha * acc_ref[...] + jnp.dot(vt, p.astype(jnp.bfloat16), preferred_element_type=jnp.float32)
    return m_new, l_new


def _gate_row(gt_ref, first_row, heads):
    rows = gt_ref[pl.ds(pl.multiple_of(first_row, V7X_SUBLANES), V7X_SUBLANES), :]
    return jnp.concatenate([rows[h:h + 1, :] for h in range(heads)], axis=1)


def _nsa_body(q_ref, kc_ref, vct_ref, ks_ref, vst_ref, kw0_ref, kw1_ref, kw2_ref, vwt0_ref, vwt1_ref, vwt2_ref,
              cstrip_ref, sstrip_ref, wstrip_ref, gt_ref, poolt_ref, o_ref, acc_ref, out_ref, p_ref, *,
              heads, n_cmp, n_sel, scale2):
    g = pl.program_id(0)
    qb = pl.program_id(1)
    t0 = qb * Q_TILE
    dh = HEAD_DIM
    lanes = heads * Q_TILE
    heads_per_chunk = V7X_MXU_WIDTH // Q_TILE
    last_tile = (t0 + Q_TILE - 1) // KEY_TILE
    qs = jnp.concatenate([q_ref[:, h * dh:(h + 1) * dh] for h in range(heads)], axis=0)
    gate_base = g * heads

    kc = kc_ref[0]
    cmp_offsets = [pl.multiple_of(jnp.clip(CMP_STRIP_ROW0 - qb * (Q_TILE // CMP_STRIDE) + m0, 0, CMP_STRIP_HI),
                                  V7X_SUBLANES) for m0 in range(0, n_cmp, V7X_SUBLANES)]
    imp = jnp.zeros((n_cmp, Q_TILE), jnp.float32)
    for c in range(heads // heads_per_chunk):
        hs = range(c * heads_per_chunk, (c + 1) * heads_per_chunk)
        lo, hi = c * V7X_MXU_WIDTH, (c + 1) * V7X_MXU_WIDTH
        s = lax.dot_general(kc, qs[lo:hi, :], _NT, preferred_element_type=jnp.float32) * scale2
        s = s + _bias_columns(cstrip_ref, hs, cmp_offsets)
        valid = s > 0.5 * NEG_INF
        m = jnp.max(s, axis=0, keepdims=True)
        e = jnp.where(valid, jnp.exp2(s - m), 0.0)
        p = e / jnp.maximum(jnp.sum(e, axis=0, keepdims=True), 1e-30)
        for k in range(heads_per_chunk):
            imp = imp + p[:, k * Q_TILE:(k + 1) * Q_TILE]
        p_ref[:, lo:hi] = p.astype(p_ref.dtype)
    gate_c = _gate_row(gt_ref, gate_base, heads)
    out_ref[...] = jnp.dot(vct_ref[0], p_ref[...], preferred_element_type=jnp.float32) * gate_c

    poolt = poolt_ref[...]
    hi_t = imp.astype(jnp.bfloat16)
    r1 = imp - hi_t.astype(jnp.float32)
    mid_t = r1.astype(jnp.bfloat16)
    lo_t = (r1 - mid_t.astype(jnp.float32)).astype(jnp.bfloat16)
    imp_sel = (jnp.dot(poolt, hi_t, preferred_element_type=jnp.float32)
               + jnp.dot(poolt, mid_t, preferred_element_type=jnp.float32)
               + jnp.dot(poolt, lo_t, preferred_element_type=jnp.float32))

    j = lax.broadcasted_iota(jnp.int32, (n_sel, Q_TILE), 0)
    t = t0 + lax.broadcasted_iota(jnp.int32, (n_sel, Q_TILE), 1)
    cur = jnp.right_shift(t, SEL_BLOCK_LOG2)
    forced = jnp.logical_or(j == 0, jnp.logical_or(j == cur, j == cur - 1))
    score = jnp.where(j * SEL_BLOCK <= t, imp_sel + jnp.where(forced, FORCE_SCORE, 0.0), NEG_INF)
    picked = jnp.zeros((n_sel, Q_TILE), jnp.float32)
    jf = j.astype(jnp.float32)
    for _ in range(min(N_SELECT, n_sel)):
        best = jnp.max(score, axis=0, keepdims=True)
        first = jnp.min(jnp.where(score == best, jf, float(n_sel)), axis=0, keepdims=True)
        hit = jf == first
        picked = jnp.where(hit, 1.0, picked)
        score = jnp.where(hit, BELOW_NEG_INF, score)
    sel_t = picked.astype(jnp.bfloat16)

    key_blk = jnp.right_shift(lax.broadcasted_iota(jnp.int32, (KEY_TILE, n_sel), 0), SEL_BLOCK_LOG2)
    blk_minus_key = lax.broadcasted_iota(jnp.int32, (KEY_TILE, n_sel), 1) - key_blk
    all_heads = range(heads)

    def sel_step(kt, carry):
        m, l = carry
        start = pl.multiple_of(kt * KEY_TILE, KEY_TILE)
        k = ks_ref[pl.ds(start, KEY_TILE), :]
        vt = vst_ref[:, pl.ds(start, KEY_TILE)]
        expand = jnp.where(blk_minus_key == kt * (KEY_TILE // SEL_BLOCK), 1.0, 0.0).astype(jnp.bfloat16)
        chosen = jnp.dot(expand, sel_t, preferred_element_type=jnp.float32)
        mask = (chosen - 1.0) * (-NEG_INF)
        row0 = SEL_STRIP_CONST - (t0 - kt * KEY_TILE)
        offsets = [pl.multiple_of(jnp.maximum(row0 + i0, 0), V7X_SUBLANES) for i0 in range(0, KEY_TILE, V7X_SUBLANES)]
        s = lax.dot_general(k, qs, _NT, preferred_element_type=jnp.float32) * scale2
        s = s + (_bias_columns(sstrip_ref, all_heads, offsets) + jnp.concatenate([mask] * heads, axis=1))
        return _online_softmax_tile(s, vt, m, l, acc_ref)

    acc_ref[...] = jnp.zeros_like(acc_ref)
    init = (jnp.full((1, lanes), NEG_INF, jnp.float32), jnp.zeros((1, lanes), jnp.float32))
    _, l_s = lax.fori_loop(0, last_tile + 1, sel_step, init)
    gate_s = _gate_row(gt_ref, N_KV_GROUPS * heads + gate_base, heads)
    out_ref[...] += acc_ref[...] * (gate_s / l_s)

    acc_ref[...] = jnp.zeros_like(acc_ref)
    m, l = init
    for back, (kw_ref, vwt_ref) in enumerate(((kw0_ref, vwt0_ref), (kw1_ref, vwt1_ref), (kw2_ref, vwt2_ref))):
        kt = last_tile - back
        row0 = WIN_STRIP_CONST - (t0 - kt * KEY_TILE)
        offsets = [pl.multiple_of(jnp.maximum(row0 + i0, 0), V7X_SUBLANES) for i0 in range(0, KEY_TILE, V7X_SUBLANES)]
        s = lax.dot_general(kw_ref[...], qs, _NT, preferred_element_type=jnp.float32) * scale2
        s = s + _bias_columns(wstrip_ref, all_heads, offsets) + jnp.where(kt < 0, NEG_INF, 0.0)
        m, l = _online_softmax_tile(s, vwt_ref[...], m, l, acc_ref)
    gate_w = _gate_row(gt_ref, 2 * N_KV_GROUPS * heads + gate_base, heads)
    out_t = out_ref[...] + acc_ref[...] * (gate_w / l)
    for h in range(heads):
        o_ref[:, h * dh:(h + 1) * dh] = out_t[:, h * Q_TILE:(h + 1) * Q_TILE].T.astype(o_ref.dtype)


def _pool_matrix_t(n_sel, n_cmp):
    r = SEL_BLOCK // CMP_STRIDE
    c = CMP_BLOCK // CMP_STRIDE
    j = jnp.arange(n_sel)[:, None]
    m = jnp.arange(n_cmp)[None, :]
    lo = r * j - (c - 1)
    return jnp.logical_and(m >= lo, m <= lo + r + c - 2).astype(jnp.bfloat16)


def nsa_attention(q, k_c, vt_c, k_sw, vt_sw, gates_t, strips, *, scale2):
    s, dq = q.shape
    dh = HEAD_DIM
    heads = dq // dh // N_KV_GROUPS
    assert heads == V7X_SUBLANES, "gate rows of one group must fill one sublane tile"
    gw = heads * dh
    n_cmp = k_c.shape[1]
    n_sel = s // SEL_BLOCK
    sel_strip, win_strip, cmp_strip = strips
    once_per_group = pl.Buffered(1)

    def last_tile(qb):
        return (qb * Q_TILE + Q_TILE - 1) // KEY_TILE

    def kw_spec(back):
        return pl.BlockSpec((KEY_TILE, dh), lambda g, qb: (jnp.maximum(last_tile(qb) - back, 0), N_KV_GROUPS + g))

    def vwt_spec(back):
        return pl.BlockSpec((dh, KEY_TILE), lambda g, qb: (N_KV_GROUPS + g, jnp.maximum(last_tile(qb) - back, 0)))

    def strip_spec(strip):
        return pl.BlockSpec((heads, strip.shape[1], V7X_LANES), lambda g, qb: (g, 0, 0), pipeline_mode=once_per_group)

    return pl.pallas_call(
        functools.partial(_nsa_body, heads=heads, n_cmp=n_cmp, n_sel=n_sel, scale2=scale2),
        grid=(N_KV_GROUPS, s // Q_TILE),
        in_specs=[
            pl.BlockSpec((Q_TILE, gw), lambda g, qb: (qb, g)),
            pl.BlockSpec((1, n_cmp, dh), lambda g, qb: (g, 0, 0)),
            pl.BlockSpec((1, dh, n_cmp), lambda g, qb: (g, 0, 0)),
            pl.BlockSpec((s, dh), lambda g, qb: (0, g), pipeline_mode=once_per_group),
            pl.BlockSpec((dh, s), lambda g, qb: (g, 0), pipeline_mode=once_per_group),
            kw_spec(0), kw_spec(1), kw_spec(2), vwt_spec(0), vwt_spec(1), vwt_spec(2),
            strip_spec(cmp_strip), strip_spec(sel_strip), strip_spec(win_strip),
            pl.BlockSpec((gates_t.shape[0], Q_TILE), lambda g, qb: (0, qb)),
            pl.BlockSpec((n_sel, n_cmp), lambda g, qb: (0, 0)),
        ],
        out_specs=pl.BlockSpec((Q_TILE, gw), lambda g, qb: (qb, g)),
        out_shape=jax.ShapeDtypeStruct((s, dq), jnp.bfloat16),
        scratch_shapes=[pltpu.VMEM((dh, heads * Q_TILE), jnp.float32), pltpu.VMEM((dh, heads * Q_TILE), jnp.float32),
                        pltpu.VMEM((n_cmp, heads * Q_TILE), jnp.bfloat16)],
        compiler_params=_compiler_params(("arbitrary", "arbitrary")),
        name="nsa_attention",
    )(q, k_c, vt_c, k_sw, vt_sw, k_sw, k_sw, k_sw, vt_sw, vt_sw, vt_sw, cmp_strip, sel_strip, win_strip, gates_t,
      _pool_matrix_t(n_sel, n_cmp))


def _bf16(w):
    return w.astype(jnp.bfloat16)


def nsa_mixer(h, w_in, cmp_pos, cmp_w1, cmp_w2, w_out, strips):
    dh, g = HEAD_DIM, N_KV_GROUPS
    q_end = w_out.shape[0]
    half = g * dh
    kv_end = q_end + N_BRANCHES * 2 * half
    n_gates = w_in.shape[1] - kv_end
    scale2 = dh ** -0.5 * LOG2_E

    def cols(branch, is_v):
        lo = q_end + branch * 2 * half + (half if is_v else 0)
        return w_in[:, lo:lo + half]

    q = matmul(h, _bf16(w_in[:, :q_end]), jnp.bfloat16, name="nsa_q_proj")
    kv_cmp = matmul(h, _bf16(w_in[:, q_end:q_end + 2 * half]), jnp.float32, name="nsa_kv_cmp_proj")
    k_sw = matmul(h, _bf16(jnp.concatenate([cols(1, False), cols(2, False)], axis=1)), jnp.bfloat16,
                  name="nsa_k_proj")
    vt_sw = matmul_nt(_bf16(jnp.concatenate([cols(1, True), cols(2, True)], axis=1).T), h, jnp.bfloat16,
                      name="nsa_vt_proj")
    w_gate_t = jnp.pad(_bf16(w_in[:, kv_end:].T), ((0, V7X_LANES - n_gates), (0, 0)))
    gates_t = matmul_nt(w_gate_t, h, jnp.float32, sigmoid=True, name="nsa_gate_proj")

    kv_c = compress_kv(kv_cmp, cmp_pos, _bf16(cmp_w1), _bf16(cmp_w2))
    k_c = kv_c[:g]
    vt_c = jnp.swapaxes(kv_c[g:], 1, 2)
    o = nsa_attention(q, k_c, vt_c, k_sw, vt_sw, gates_t, strips, scale2=scale2)
    return matmul(o, _bf16(w_out), jnp.float32, name="nsa_out_proj")


def pool_mixer(h, w_in, w_group, scale, w_out):
    y = pool_front(h, _bf16(w_in), _bf16(w_group), scale)
    return matmul(y, _bf16(w_out), jnp.float32, name="pool_out_proj")


def kernel(x, norm_gains, ffn_w_gate, ffn_w_up, ffn_w_down, pool_w_in, pool_w_group, pool_scale, pool_w_out,
           nsa_w_in, nsa_cmp_pos, nsa_cmp_w1, nsa_cmp_w2, nsa_w_out, rel_bias):
    b, s, d = x.shape
    depth = norm_gains.shape[0]
    outs = []
    strips = nsa_strips(rel_bias) if depth > 1 else None
    for bi in range(b):
        xs = x[bi]
        h = rms_norm_cast(xs, norm_gains[0, 0])
        for i in range(depth):
            gains = norm_gains[i]
            f1 = swiglu_ffn(h, _bf16(ffn_w_gate[i, 0]), _bf16(ffn_w_up[i, 0]), _bf16(ffn_w_down[i, 0]))
            xs, h = residual_norm(xs, f1, gains[1], gains[2], 0.5)
            li = i // 2
            if i % 2 == 0:
                m = pool_mixer(h, pool_w_in[li], pool_w_group[li], pool_scale[li], pool_w_out[li])
            else:
                m = nsa_mixer(h, nsa_w_in[li], nsa_cmp_pos[li], nsa_cmp_w1[li], nsa_cmp_w2[li], nsa_w_out[li],
                              strips)
            xs, h = residual_norm(xs, m, gains[3], gains[4], 1.0)
            f2 = swiglu_ffn(h, _bf16(ffn_w_gate[i, 1]), _bf16(ffn_w_up[i, 1]), _bf16(ffn_w_down[i, 1]))
            g_next = norm_gains[i + 1, 0] if i + 1 < depth else None
            xs, h = residual_norm(xs, f2, gains[5], g_next, 0.5)
        outs.append(xs)
    return jnp.stack(outs, axis=0)
```

```python
import functools
import math

import jax
import jax.numpy as jnp
from jax import lax
from jax.experimental import pallas as pl
from jax.experimental.pallas import tpu as pltpu

NORM_EPS = 1e-6
POOL_WINDOWS = (2, 4, 8, 16)
HEAD_DIM = 128
N_KV_GROUPS = 4
N_BRANCHES = 3
CMP_BLOCK = 32
CMP_STRIDE = 16
SEL_BLOCK = 64
SEL_BLOCK_LOG2 = 6
N_SELECT = 16
WINDOW = 512
N_BUCKETS = 32
MAX_EXACT = N_BUCKETS // 2
FORCE_SCORE = 1e6
NEG_INF = -1e30
BELOW_NEG_INF = -3e38
LOG2_E = math.log2(math.e)

V7X_LANES = 128
V7X_SUBLANES = 8
V7X_MXU_WIDTH = 256
V7X_VMEM_LIMIT_BYTES = 56 * 1024 * 1024

Q_TILE = 128
KEY_TILE = 512
WIN_TILE = 256
AUG_ROWS = 16


def _compiler_params(semantics):
    return pltpu.CompilerParams(dimension_semantics=semantics, vmem_limit_bytes=V7X_VMEM_LIMIT_BYTES)


def _round_up(x, m):
    return -(-x // m) * m


def _rms(xf, gain):
    ms = jnp.mean(xf * xf, axis=-1, keepdims=True)
    return xf * lax.rsqrt(ms + NORM_EPS) * gain


_NT = (((1,), (1,)), ((), ()))


def _norm_body(x_ref, g_ref, h_ref):
    h_ref[...] = _rms(x_ref[...], g_ref[...]).astype(h_ref.dtype)


def rms_norm_cast(x, gain, *, rows=256):
    s, d = x.shape
    return pl.pallas_call(
        _norm_body,
        grid=(s // rows,),
        in_specs=[pl.BlockSpec((rows, d), lambda i: (i, 0)), pl.BlockSpec((1, d), lambda i: (0, 0))],
        out_specs=pl.BlockSpec((rows, d), lambda i: (i, 0)),
        out_shape=jax.ShapeDtypeStruct((s, d), jnp.bfloat16),
        compiler_params=_compiler_params(("arbitrary",)),
        name="rms_norm_cast",
    )(x, gain.reshape(1, d))


def _resnorm_body(x_ref, f_ref, gp_ref, gn_ref, xo_ref, h_ref, *, alpha):
    y = x_ref[...] + alpha * _rms(f_ref[...], gp_ref[...])
    xo_ref[...] = y
    h_ref[...] = _rms(y, gn_ref[...]).astype(h_ref.dtype)


def _resnorm_last_body(x_ref, f_ref, gp_ref, xo_ref, *, alpha):
    xo_ref[...] = x_ref[...] + alpha * _rms(f_ref[...], gp_ref[...])


def residual_norm(x, f, g_post, g_next, alpha, *, rows=256):
    s, d = x.shape
    row_spec = pl.BlockSpec((rows, d), lambda i: (i, 0))
    gain_spec = pl.BlockSpec((1, d), lambda i: (0, 0))
    if g_next is None:
        return pl.pallas_call(
            functools.partial(_resnorm_last_body, alpha=alpha),
            grid=(s // rows,),
            in_specs=[row_spec, row_spec, gain_spec],
            out_specs=row_spec,
            out_shape=jax.ShapeDtypeStruct((s, d), jnp.float32),
            compiler_params=_compiler_params(("arbitrary",)),
            name="residual_norm_last",
        )(x, f, g_post.reshape(1, d)), None
    return pl.pallas_call(
        functools.partial(_resnorm_body, alpha=alpha),
        grid=(s // rows,),
        in_specs=[row_spec, row_spec, gain_spec, gain_spec],
        out_specs=[row_spec, row_spec],
        out_shape=[jax.ShapeDtypeStruct((s, d), jnp.float32), jax.ShapeDtypeStruct((s, d), jnp.bfloat16)],
        compiler_params=_compiler_params(("arbitrary",)),
        name="residual_norm",
    )(x, f, g_post.reshape(1, d), g_next.reshape(1, d))


def _ffn_body(h_ref, wg_ref, wu_ref, wd_ref, o_ref, a_ref, *, n_up, tf):
    j = pl.program_id(1)

    @pl.when(j < n_up)
    def _():
        h = h_ref[...]
        g = jnp.dot(h, wg_ref[...], preferred_element_type=jnp.float32)
        u = jnp.dot(h, wu_ref[...], preferred_element_type=jnp.float32)
        a = (g * jax.nn.sigmoid(g)) * u
        a_ref[:, pl.ds(pl.multiple_of(j * tf, tf), tf)] = a.astype(a_ref.dtype)

    @pl.when(j >= n_up)
    def _():
        o_ref[...] = jnp.dot(a_ref[...], wd_ref[...], preferred_element_type=jnp.float32)


def swiglu_ffn(h, w_gate, w_up, w_down, *, tm=1024, tf=256, tn=256):
    s, d = h.shape
    f = w_gate.shape[1]
    tm = min(tm, s)
    n_up, n_down = f // tf, d // tn
    return pl.pallas_call(
        functools.partial(_ffn_body, n_up=n_up, tf=tf),
        grid=(s // tm, n_up + n_down),
        in_specs=[
            pl.BlockSpec((tm, d), lambda i, j: (i, 0)),
            pl.BlockSpec((d, tf), lambda i, j: (0, jnp.minimum(j, n_up - 1))),
            pl.BlockSpec((d, tf), lambda i, j: (0, jnp.minimum(j, n_up - 1))),
            pl.BlockSpec((f, tn), lambda i, j: (0, jnp.maximum(j - n_up, 0))),
        ],
        out_specs=pl.BlockSpec((tm, tn), lambda i, j: (i, jnp.maximum(j - n_up, 0))),
        out_shape=jax.ShapeDtypeStruct((s, d), jnp.float32),
        scratch_shapes=[pltpu.VMEM((tm, f), jnp.bfloat16)],
        compiler_params=_compiler_params(("arbitrary", "arbitrary")),
        name="swiglu_ffn",
    )(h, w_gate, w_up, w_down)


def _matmul_body(a_ref, b_ref, o_ref):
    o_ref[...] = jnp.dot(a_ref[...], b_ref[...], preferred_element_type=jnp.float32).astype(o_ref.dtype)


def matmul(a, b, out_dtype, *, tm=1024, tn=512, name="matmul"):
    m, k = a.shape
    n = b.shape[1]
    tm, tn = min(tm, m), min(tn, n)
    return pl.pallas_call(
        _matmul_body,
        grid=(m // tm, n // tn),
        in_specs=[pl.BlockSpec((tm, k), lambda i, j: (i, 0)), pl.BlockSpec((k, tn), lambda i, j: (0, j))],
        out_specs=pl.BlockSpec((tm, tn), lambda i, j: (i, j)),
        out_shape=jax.ShapeDtypeStruct((m, n), out_dtype),
        compiler_params=_compiler_params(("arbitrary", "arbitrary")),
        name=name,
    )(a, b)


def _matmul_nt_body(bt_ref, a_ref, o_ref, *, sigmoid, out_scale):
    r = lax.dot_general(bt_ref[...], a_ref[...], _NT, preferred_element_type=jnp.float32)
    if sigmoid:
        r = jax.nn.sigmoid(r)
    if out_scale is not None:
        r = r * out_scale
    o_ref[...] = r.astype(o_ref.dtype)


def matmul_nt(bt, a, out_dtype, *, tm=1024, tn=512, sigmoid=False, out_scale=None, name="matmul_nt"):
    n, k = bt.shape
    m = a.shape[0]
    tm, tn = min(tm, m), min(tn, n)
    return pl.pallas_call(
        functools.partial(_matmul_nt_body, sigmoid=sigmoid, out_scale=out_scale),
        grid=(m // tm, n // tn),
        in_specs=[pl.BlockSpec((tn, k), lambda i, j: (j, 0)), pl.BlockSpec((tm, k), lambda i, j: (i, 0))],
        out_specs=pl.BlockSpec((tn, tm), lambda i, j: (j, i)),
        out_shape=jax.ShapeDtypeStruct((n, m), out_dtype),
        compiler_params=_compiler_params(("arbitrary", "arbitrary")),
        name=name,
    )(bt, a)


POOL_HALO = 16


def _pool_body(h_ref, win_ref, wgrp_ref, scale_ref, y_ref, halo_ref, *, tm):
    grp = pl.program_id(0)
    i = pl.program_id(1)
    u = jnp.dot(h_ref[...], win_ref[...], preferred_element_type=jnp.float32)

    @pl.when(i == 0)
    def _():
        halo_ref[...] = jnp.zeros_like(halo_ref)

    ext = jnp.concatenate([halo_ref[...], u], axis=0)
    halo_ref[...] = u[tm - POOL_HALO:, :]
    s2 = ext + pltpu.roll(ext, 1, 0)
    s4 = s2 + pltpu.roll(s2, 2, 0)
    s8 = s4 + pltpu.roll(s4, 4, 0)
    s16 = s8 + pltpu.roll(s8, 8, 0)
    sw = jnp.where(grp == 0, s2, jnp.where(grp == 1, s4, jnp.where(grp == 2, s8, s16)))[POOL_HALO:, :]
    w = jnp.where(grp == 0, POOL_WINDOWS[0], jnp.where(grp == 1, POOL_WINDOWS[1],
                  jnp.where(grp == 2, POOL_WINDOWS[2], POOL_WINDOWS[3])))
    t = i * tm + lax.broadcasted_iota(jnp.int32, (tm, 1), 0)
    count = jnp.minimum(t + 1, w).astype(jnp.float32)
    d = sw / count - u
    y = jnp.dot(d.astype(jnp.bfloat16), wgrp_ref[0], preferred_element_type=jnp.float32)
    y_ref[...] = (y * scale_ref[...]).astype(y_ref.dtype)


def pool_front(h, w_in, w_group, scale, *, tm=512):
    s, d = h.shape
    n_groups, gw, _ = w_group.shape
    tm = min(tm, s)
    return pl.pallas_call(
        functools.partial(_pool_body, tm=tm),
        grid=(n_groups, s // tm),
        in_specs=[
            pl.BlockSpec((tm, d), lambda g, i: (i, 0)),
            pl.BlockSpec((d, gw), lambda g, i: (0, g)),
            pl.BlockSpec((1, gw, gw), lambda g, i: (g, 0, 0)),
            pl.BlockSpec((1, gw), lambda g, i: (0, g)),
        ],
        out_specs=pl.BlockSpec((tm, gw), lambda g, i: (i, g)),
        out_shape=jax.ShapeDtypeStruct((s, d), jnp.bfloat16),
        scratch_shapes=[pltpu.VMEM((POOL_HALO, gw), jnp.float32)],
        compiler_params=_compiler_params(("arbitrary", "arbitrary")),
        name="pool_front",
    )(h, w_in, w_group, scale.reshape(1, d))


def _bucket_thresholds():
    thr = list(range(1, MAX_EXACT + 1))
    for k in range(1, N_BUCKETS - MAX_EXACT):
        thr.append(math.isqrt(256 * 2 ** k - 1) + 1)
    return tuple(thr)


BUCKET_THRESHOLDS = _bucket_thresholds()
LAST_BUCKET_DIST = BUCKET_THRESHOLDS[-1]
STRIP_ROW_BLOCK = 256


def _strip_body(tab_ref, o_ref, *, rows, row_step, const, max_dist):
    h = pl.program_id(0)
    r = pl.program_id(1)
    a = r * rows + lax.broadcasted_iota(jnp.int32, (rows, V7X_LANES), 0)
    l = lax.broadcasted_iota(jnp.int32, (rows, V7X_LANES), 1)
    dist = l + const - row_step * a
    v = jnp.full((rows, V7X_LANES), tab_ref[0, h], jnp.float32)
    for b, thr in enumerate(BUCKET_THRESHOLDS, start=1):
        v = jnp.where(dist >= thr, tab_ref[b, h], v)
    ok = dist >= 0
    if max_dist is not None:
        ok = jnp.logical_and(ok, dist < max_dist)
    o_ref[0] = jnp.where(ok, v * LOG2_E, NEG_INF)


def bias_strip(rel_table, min_rows, *, row_step, const, max_dist):
    n_heads = rel_table.shape[1]
    rows = STRIP_ROW_BLOCK
    n_rows = _round_up(min_rows, rows)
    return pl.pallas_call(
        functools.partial(_strip_body, rows=rows, row_step=row_step, const=const, max_dist=max_dist),
        grid=(n_heads, n_rows // rows),
        in_specs=[pl.BlockSpec(memory_space=pltpu.SMEM)],
        out_specs=pl.BlockSpec((1, rows, V7X_LANES), lambda h, r: (h, r, 0)),
        out_shape=jax.ShapeDtypeStruct((n_heads, n_rows, V7X_LANES), jnp.float32),
        compiler_params=_compiler_params(("arbitrary", "arbitrary")),
        name="bias_strip",
    )(rel_table)


SEL_STRIP_CONST = _round_up(LAST_BUCKET_DIST + V7X_SUBLANES - 1, V7X_SUBLANES)
SEL_STRIP_ROWS = SEL_STRIP_CONST + KEY_TILE
WIN_STRIP_CONST = _round_up(WINDOW + V7X_SUBLANES - 1, V7X_SUBLANES)
WIN_STRIP_ROWS = WIN_STRIP_CONST + WIN_TILE
CMP_DIST_OFFSET = CMP_BLOCK - 1
CMP_STRIP_CONST = _round_up(LAST_BUCKET_DIST + CMP_STRIDE * (V7X_SUBLANES - 1) + CMP_DIST_OFFSET,
                            CMP_STRIDE * V7X_SUBLANES) - CMP_DIST_OFFSET
CMP_STRIP_ROW0 = (CMP_STRIP_CONST + CMP_DIST_OFFSET) // CMP_STRIDE
CMP_STRIP_HI = _round_up((V7X_LANES + CMP_STRIP_CONST) // CMP_STRIDE + 1, V7X_SUBLANES)
CMP_STRIP_ROWS = CMP_STRIP_HI + V7X_SUBLANES


def nsa_strips(rel_bias):
    sel_strip = bias_strip(rel_bias, SEL_STRIP_ROWS, row_step=1, const=SEL_STRIP_CONST, max_dist=None)
    win_strip = bias_strip(rel_bias, WIN_STRIP_ROWS, row_step=1, const=WIN_STRIP_CONST, max_dist=WINDOW)
    cmp_strip = bias_strip(rel_bias, CMP_STRIP_ROWS, row_step=CMP_STRIDE, const=CMP_STRIP_CONST, max_dist=None)
    return sel_strip, win_strip, cmp_strip


def _bias_columns(strip_ref, heads, offsets):
    cols = [jnp.concatenate([strip_ref[h, pl.ds(e, V7X_SUBLANES), :] for e in offsets], axis=0) for h in heads]
    return jnp.concatenate(cols, axis=1) if len(cols) > 1 else cols[0]


def _compress_body(x_ref, pos_ref, w1a_ref, w1b_ref, w2_ref, o_ref, acca_ref, accb_ref, *, n_chunks):
    l = pl.program_id(1)

    @pl.when(l == 0)
    def _():
        acca_ref[...] = jnp.zeros_like(acca_ref)
        accb_ref[...] = jnp.zeros_like(accb_ref)

    x = x_ref[...]
    pa = pos_ref[0, pl.ds(l, 1), :]
    pb = pos_ref[0, pl.ds(l + CMP_STRIDE, 1), :]
    acca_ref[...] += jnp.dot((x + pa).astype(jnp.bfloat16), w1a_ref[0], preferred_element_type=jnp.float32)
    accb_ref[...] += jnp.dot((x + pb).astype(jnp.bfloat16), w1b_ref[0], preferred_element_type=jnp.float32)

    @pl.when(l == CMP_STRIDE - 1)
    def _():
        pre = acca_ref[...] + pltpu.roll(accb_ref[...], n_chunks - 1, 0)
        act = jax.nn.gelu(pre)
        o_ref[0] = jnp.dot(act.astype(jnp.bfloat16), w2_ref[0],
                           preferred_element_type=jnp.float32).astype(o_ref.dtype)


def compress_kv(kv_cmp, pos, w1, w2):
    s = kv_cmp.shape[0]
    dh = HEAD_DIM
    n_chunks = s // CMP_STRIDE
    n_kvg = 2 * N_KV_GROUPS
    x = kv_cmp.reshape(n_chunks, CMP_STRIDE * n_kvg * dh)
    w1r = w1.reshape(2, CMP_BLOCK, dh, dh)
    return pl.pallas_call(
        functools.partial(_compress_body, n_chunks=n_chunks),
        grid=(n_kvg, CMP_STRIDE),
        in_specs=[
            pl.BlockSpec((n_chunks, dh), lambda c, l: (0, l * n_kvg + c)),
            pl.BlockSpec((1, CMP_BLOCK, dh), lambda c, l: (c // N_KV_GROUPS, 0, 0)),
            pl.BlockSpec((1, None, dh, dh), lambda c, l: (c // N_KV_GROUPS, l, 0, 0)),
            pl.BlockSpec((1, None, dh, dh), lambda c, l: (c // N_KV_GROUPS, l + CMP_STRIDE, 0, 0)),
            pl.BlockSpec((1, dh, dh), lambda c, l: (c // N_KV_GROUPS, 0, 0)),
        ],
        out_specs=pl.BlockSpec((1, n_chunks, dh), lambda c, l: (c, 0, 0)),
        out_shape=jax.ShapeDtypeStruct((n_kvg, n_chunks, dh), jnp.bfloat16),
        scratch_shapes=[pltpu.VMEM((n_chunks, dh), jnp.float32), pltpu.VMEM((n_chunks, dh), jnp.float32)],
        compiler_params=_compiler_params(("arbitrary", "arbitrary")),
        name="compress_kv",
    )(x, pos, w1r, w1r, w2)


def _online_softmax_tile(s, vt, m, acc_ref, skip=None):
    m_new = jnp.maximum(m, jnp.max(s, axis=0, keepdims=True))
    alpha = jnp.exp2(m - m_new)
    m_sub = m_new
    if skip is not None:
        m_new = jnp.where(skip, m, m_new)
        alpha = jnp.where(skip, 1.0, alpha)
        m_sub = jnp.where(skip, -NEG_INF, m_new)
    p = jnp.exp2(s - m_sub).astype(jnp.bfloat16)
    acc_ref[...] = alpha * acc_ref[...] + jnp.dot(vt, p, preferred_element_type=jnp.float32)
    return m_new


def _ones_rows(keys):
    row = lax.broadcasted_iota(jnp.int32, (AUG_ROWS, keys), 0)
    return jnp.where(row == 0, 1.0, 0.0).astype(jnp.bfloat16)


def _gate_row(gt_ref, first_row, heads):
    rows = gt_ref[pl.ds(pl.multiple_of(first_row, V7X_SUBLANES), V7X_SUBLANES), :]
    return jnp.concatenate([rows[h:h + 1, :] for h in range(heads)], axis=1)


def _nsa_body(q_ref, kc_ref, vct_ref, ks_ref, vst_ref, kw0_ref, kw1_ref, kw2_ref, vwt0_ref, vwt1_ref, vwt2_ref,
              cstrip_ref, sstrip_ref, wstrip_ref, gt_ref, poolt_ref, o_ref,
              acc_ref, out_ref, p_ref, selm_ref, sa_ref, sb_ref, *,
              heads, n_cmp, n_sel):
    g = pl.program_id(0)
    qb = pl.program_id(1)
    t0 = qb * Q_TILE
    dh = HEAD_DIM
    lanes = heads * Q_TILE
    heads_per_chunk = V7X_MXU_WIDTH // Q_TILE
    qt = jnp.concatenate([q_ref[h * dh:(h + 1) * dh, :] for h in range(heads)], axis=1)
    gate_base = g * heads

    kc = kc_ref[0]
    cmp_offsets = [pl.multiple_of(jnp.clip(CMP_STRIP_ROW0 - qb * (Q_TILE // CMP_STRIDE) + m0, 0, CMP_STRIP_HI),
                                  V7X_SUBLANES) for m0 in range(0, n_cmp, V7X_SUBLANES)]
    imp = jnp.zeros((n_cmp, Q_TILE), jnp.float32)
    for c in range(heads // heads_per_chunk):
        hs = range(c * heads_per_chunk, (c + 1) * heads_per_chunk)
        lo, hi = c * V7X_MXU_WIDTH, (c + 1) * V7X_MXU_WIDTH
        s = jnp.dot(kc, qt[:, lo:hi], preferred_element_type=jnp.float32)
        s = s + _bias_columns(cstrip_ref, hs, cmp_offsets)
        m = jnp.max(s, axis=0, keepdims=True)
        e = jnp.exp2(s - m)
        l = jnp.maximum(jnp.sum(e, axis=0, keepdims=True), 1e-30)
        p = e * jnp.where(m > 0.5 * NEG_INF, 1.0 / l, 0.0)
        for k in range(heads_per_chunk):
            imp = imp + p[:, k * Q_TILE:(k + 1) * Q_TILE]
        p_ref[:, lo:hi] = p.astype(p_ref.dtype)
    gate_c = _gate_row(gt_ref, gate_base, heads)
    out_ref[...] = jnp.dot(vct_ref[0], p_ref[...], preferred_element_type=jnp.float32) * gate_c

    poolt = poolt_ref[...]
    hi_t = imp.astype(jnp.bfloat16)
    r1 = imp - hi_t.astype(jnp.float32)
    mid_t = r1.astype(jnp.bfloat16)
    lo_t = (r1 - mid_t.astype(jnp.float32)).astype(jnp.bfloat16)
    imp_sel = (jnp.dot(poolt, hi_t, preferred_element_type=jnp.float32)
               + jnp.dot(poolt, mid_t, preferred_element_type=jnp.float32)
               + jnp.dot(poolt, lo_t, preferred_element_type=jnp.float32))

    j = lax.broadcasted_iota(jnp.int32, (n_sel, Q_TILE), 0)
    t = t0 + lax.broadcasted_iota(jnp.int32, (n_sel, Q_TILE), 1)
    cur = jnp.right_shift(t, SEL_BLOCK_LOG2)
    forced = jnp.logical_or(j == 0, jnp.logical_or(j == cur, j == cur - 1))
    score = jnp.where(j * SEL_BLOCK <= t, imp_sel + jnp.where(forced, FORCE_SCORE, 0.0), NEG_INF)
    picked = jnp.zeros((n_sel, Q_TILE), jnp.float32)
    jf = j.astype(jnp.float32)
    for _ in range(min(N_SELECT, n_sel)):
        best = jnp.max(score, axis=0, keepdims=True)
        first = jnp.min(jnp.where(score == best, jf, float(n_sel)), axis=0, keepdims=True)
        hit = jf == first
        picked = jnp.where(hit, 1.0, picked)
        score = jnp.where(hit, BELOW_NEG_INF, score)
    selm_ref[...] = jnp.concatenate([picked - 1.0] * heads, axis=1)

    blocks_per_tile = KEY_TILE // SEL_BLOCK
    key_blk = jnp.right_shift(lax.broadcasted_iota(jnp.int32, (KEY_TILE, V7X_LANES), 0), SEL_BLOCK_LOG2)
    blk_col = lax.broadcasted_iota(jnp.int32, (KEY_TILE, V7X_LANES), 1)
    key_mask_cols = jnp.where(key_blk == blk_col, -NEG_INF, 0.0).astype(jnp.bfloat16)
    zero_rows = jnp.zeros((V7X_LANES - AUG_ROWS, lanes), jnp.bfloat16)
    sel_ones = _ones_rows(KEY_TILE)
    all_heads = range(heads)

    last_tile = (t0 + Q_TILE - 1) // KEY_TILE

    def scores(kt):
        ktc = jnp.minimum(kt, last_tile)
        start = pl.multiple_of(ktc * KEY_TILE, KEY_TILE)
        k_aug = jnp.concatenate([ks_ref[pl.ds(start, KEY_TILE), :], key_mask_cols], axis=1)
        sel_rows = selm_ref[pl.ds(pl.multiple_of(ktc * blocks_per_tile, blocks_per_tile), blocks_per_tile), :]
        sel_rows = jnp.concatenate([sel_rows, jnp.zeros((AUG_ROWS - blocks_per_tile, lanes), jnp.float32)], axis=0)
        qt_aug = jnp.concatenate([qt, sel_rows.astype(jnp.bfloat16), zero_rows], axis=0)
        row0 = SEL_STRIP_CONST - (t0 - ktc * KEY_TILE)
        offsets = [pl.multiple_of(jnp.maximum(row0 + i0, 0), V7X_SUBLANES) for i0 in range(0, KEY_TILE, V7X_SUBLANES)]
        s = jnp.dot(k_aug, qt_aug, preferred_element_type=jnp.float32)
        return s + _bias_columns(sstrip_ref, all_heads, offsets)

    def values(kt):
        start = pl.multiple_of(jnp.minimum(kt, last_tile) * KEY_TILE, KEY_TILE)
        return jnp.concatenate([vst_ref[:, pl.ds(start, KEY_TILE)], sel_ones], axis=0)

    def sel_step(i, m):
        kt = 2 * i
        sb_ref[...] = scores(kt + 1)
        m = _online_softmax_tile(sa_ref[...], values(kt), m, acc_ref)
        sa_ref[...] = scores(kt + 2)
        return _online_softmax_tile(sb_ref[...], values(kt + 1), m, acc_ref, skip=kt + 1 > last_tile)

    acc_ref[...] = jnp.zeros_like(acc_ref)
    m_init = jnp.full((1, lanes), NEG_INF, jnp.float32)
    sa_ref[...] = scores(0)
    lax.fori_loop(0, last_tile // 2 + 1, sel_step, m_init)
    gate_s = _gate_row(gt_ref, N_KV_GROUPS * heads + gate_base, heads)
    out_ref[...] += acc_ref[:dh, :] * (gate_s / acc_ref[dh:dh + 1, :])

    acc_ref[...] = jnp.zeros_like(acc_ref)
    m = m_init
    win_ones = _ones_rows(WIN_TILE)
    last_win_tile = (t0 + Q_TILE - 1) // WIN_TILE
    for back, (kw_ref, vwt_ref) in enumerate(((kw0_ref, vwt0_ref), (kw1_ref, vwt1_ref), (kw2_ref, vwt2_ref))):
        kt = last_win_tile - back
        row0 = WIN_STRIP_CONST - (t0 - kt * WIN_TILE)
        offsets = [pl.multiple_of(jnp.maximum(row0 + i0, 0), V7X_SUBLANES) for i0 in range(0, WIN_TILE, V7X_SUBLANES)]
        s = jnp.dot(kw_ref[...], qt, preferred_element_type=jnp.float32)
        s = s + _bias_columns(wstrip_ref, all_heads, offsets)
        m = _online_softmax_tile(s, jnp.concatenate([vwt_ref[...], win_ones], axis=0), m, acc_ref,
                                 skip=None if back == 0 else kt < 0)
    gate_w = _gate_row(gt_ref, 2 * N_KV_GROUPS * heads + gate_base, heads)
    out_t = out_ref[...] + acc_ref[:dh, :] * (gate_w / acc_ref[dh:dh + 1, :])
    for h in range(heads):
        o_ref[:, h * dh:(h + 1) * dh] = out_t[:, h * Q_TILE:(h + 1) * Q_TILE].T.astype(o_ref.dtype)


def _pool_matrix_t(n_sel, n_cmp):
    r = SEL_BLOCK // CMP_STRIDE
    c = CMP_BLOCK // CMP_STRIDE
    j = jnp.arange(n_sel)[:, None]
    m = jnp.arange(n_cmp)[None, :]
    lo = r * j - (c - 1)
    return jnp.logical_and(m >= lo, m <= lo + r + c - 2).astype(jnp.bfloat16)


def nsa_attention(q_t, k_c, vt_c, k_sw, vt_sw, gates_t, strips):
    dq, s = q_t.shape
    dh = HEAD_DIM
    heads = dq // dh // N_KV_GROUPS
    assert heads == V7X_SUBLANES, "gate rows of one group must fill one sublane tile"
    assert KEY_TILE // SEL_BLOCK == V7X_SUBLANES, "one key tile's selection rows must fill one sublane tile"
    gw = heads * dh
    n_cmp = k_c.shape[1]
    n_sel = s // SEL_BLOCK
    sel_strip, win_strip, cmp_strip = strips
    once_per_group = pl.Buffered(1)

    def last_tile(qb):
        return (qb * Q_TILE + Q_TILE - 1) // WIN_TILE

    def kw_spec(back):
        return pl.BlockSpec((WIN_TILE, dh), lambda g, qb: (jnp.maximum(last_tile(qb) - back, 0), N_KV_GROUPS + g))

    def vwt_spec(back):
        return pl.BlockSpec((dh, WIN_TILE), lambda g, qb: (N_KV_GROUPS + g, jnp.maximum(last_tile(qb) - back, 0)))

    def strip_spec(strip):
        return pl.BlockSpec((heads, strip.shape[1], V7X_LANES), lambda g, qb: (g, 0, 0), pipeline_mode=once_per_group)

    return pl.pallas_call(
        functools.partial(_nsa_body, heads=heads, n_cmp=n_cmp, n_sel=n_sel),
        grid=(N_KV_GROUPS, s // Q_TILE),
        in_specs=[
            pl.BlockSpec((gw, Q_TILE), lambda g, qb: (g, qb)),
            pl.BlockSpec((1, n_cmp, dh), lambda g, qb: (g, 0, 0)),
            pl.BlockSpec((1, dh, n_cmp), lambda g, qb: (g, 0, 0)),
            pl.BlockSpec((s, dh), lambda g, qb: (0, g), pipeline_mode=once_per_group),
            pl.BlockSpec((dh, s), lambda g, qb: (g, 0), pipeline_mode=once_per_group),
            kw_spec(0), kw_spec(1), kw_spec(2), vwt_spec(0), vwt_spec(1), vwt_spec(2),
            strip_spec(cmp_strip), strip_spec(sel_strip), strip_spec(win_strip),
            pl.BlockSpec((gates_t.shape[0], Q_TILE), lambda g, qb: (0, qb)),
            pl.BlockSpec((n_sel, n_cmp), lambda g, qb: (0, 0)),
        ],
        out_specs=pl.BlockSpec((Q_TILE, gw), lambda g, qb: (qb, g)),
        out_shape=jax.ShapeDtypeStruct((s, dq), jnp.bfloat16),
        scratch_shapes=[pltpu.VMEM((dh + AUG_ROWS, heads * Q_TILE), jnp.float32),
                        pltpu.VMEM((dh, heads * Q_TILE), jnp.float32),
                        pltpu.VMEM((n_cmp, heads * Q_TILE), jnp.bfloat16),
                        pltpu.VMEM((n_sel, heads * Q_TILE), jnp.float32),
                        pltpu.VMEM((KEY_TILE, heads * Q_TILE), jnp.float32),
                        pltpu.VMEM((KEY_TILE, heads * Q_TILE), jnp.float32)],
        compiler_params=_compiler_params(("arbitrary", "arbitrary")),
        name="nsa_attention",
    )(q_t, k_c, vt_c, k_sw, vt_sw, k_sw, k_sw, k_sw, vt_sw, vt_sw, vt_sw, cmp_strip, sel_strip, win_strip, gates_t,
      _pool_matrix_t(n_sel, n_cmp))


def _bf16(w):
    return w.astype(jnp.bfloat16)


def nsa_mixer(h, w_in, cmp_pos, cmp_w1, cmp_w2, w_out, strips):
    dh, g = HEAD_DIM, N_KV_GROUPS
    q_end = w_out.shape[0]
    half = g * dh
    kv_end = q_end + N_BRANCHES * 2 * half
    n_gates = w_in.shape[1] - kv_end
    scale2 = dh ** -0.5 * LOG2_E

    def cols(branch, is_v):
        lo = q_end + branch * 2 * half + (half if is_v else 0)
        return w_in[:, lo:lo + half]

    q_t = matmul_nt(_bf16(w_in[:, :q_end].T), h, jnp.bfloat16, out_scale=scale2, name="nsa_q_proj")
    kv_cmp = matmul(h, _bf16(w_in[:, q_end:q_end + 2 * half]), jnp.float32, name="nsa_kv_cmp_proj")
    k_sw = matmul(h, _bf16(jnp.concatenate([cols(1, False), cols(2, False)], axis=1)), jnp.bfloat16,
                  name="nsa_k_proj")
    vt_sw = matmul_nt(_bf16(jnp.concatenate([cols(1, True), cols(2, True)], axis=1).T), h, jnp.bfloat16,
                      name="nsa_vt_proj")
    w_gate_t = jnp.pad(_bf16(w_in[:, kv_end:].T), ((0, V7X_LANES - n_gates), (0, 0)))
    gates_t = matmul_nt(w_gate_t, h, jnp.float32, sigmoid=True, name="nsa_gate_proj")

    kv_c = compress_kv(kv_cmp, cmp_pos, _bf16(cmp_w1), _bf16(cmp_w2))
    k_c = kv_c[:g]
    vt_c = jnp.swapaxes(kv_c[g:], 1, 2)
    o = nsa_attention(q_t, k_c, vt_c, k_sw, vt_sw, gates_t, strips)
    return matmul(o, _bf16(w_out), jnp.float32, name="nsa_out_proj")


def pool_mixer(h, w_in, w_group, scale, w_out):
    y = pool_front(h, _bf16(w_in), _bf16(w_group), scale)
    return matmul(y, _bf16(w_out), jnp.float32, name="pool_out_proj")


def kernel(x, norm_gains, ffn_w_gate, ffn_w_up, ffn_w_down, pool_w_in, pool_w_group, pool_scale, pool_w_out,
           nsa_w_in, nsa_cmp_pos, nsa_cmp_w1, nsa_cmp_w2, nsa_w_out, rel_bias):
    b, s, d = x.shape
    depth = norm_gains.shape[0]
    outs = []
    strips = nsa_strips(rel_bias) if depth > 1 else None
    for bi in range(b):
        xs = x[bi]
        h = rms_norm_cast(xs, norm_gains[0, 0])
        for i in range(depth):
            gains = norm_gains[i]
            f1 = swiglu_ffn(h, _bf16(ffn_w_gate[i, 0]), _bf16(ffn_w_up[i, 0]), _bf16(ffn_w_down[i, 0]))
            xs, h = residual_norm(xs, f1, gains[1], gains[2], 0.5)
            li = i // 2
            if i % 2 == 0:
                m = pool_mixer(h, pool_w_in[li], pool_w_group[li], pool_scale[li], pool_w_out[li])
            else:
                m = nsa_mixer(h, nsa_w_in[li], nsa_cmp_pos[li], nsa_cmp_w1[li], nsa_cmp_w2[li], nsa_w_out[li],
                              strips)
            xs, h = residual_norm(xs, m, gains[3], gains[4], 1.0)
            f2 = swiglu_ffn(h, _bf16(ffn_w_gate[i, 1]), _bf16(ffn_w_up[i, 1]), _bf16(ffn_w_down[i, 1]))
            g_next = norm_gains[i + 1, 0] if i + 1 < depth else None
            xs, h = residual_norm(xs, f2, gains[5], g_next, 0.5)
        outs.append(xs)
    return jnp.stack(outs, axis=0)
```

```python
import functools
import math

import jax
import jax.numpy as jnp
from jax import lax
from jax.experimental import pallas as pl
from jax.experimental.pallas import tpu as pltpu

NORM_EPS = 1e-6
POOL_WINDOWS = (2, 4, 8, 16)
HEAD_DIM = 128
N_KV_GROUPS = 4
N_BRANCHES = 3
CMP_BLOCK = 32
CMP_STRIDE = 16
SEL_BLOCK = 64
SEL_BLOCK_LOG2 = 6
N_SELECT = 16
WINDOW = 512
N_BUCKETS = 32
MAX_EXACT = N_BUCKETS // 2
NEG_INF = -1e30
BELOW_NEG_INF = -3e38
LOG2_E = math.log2(math.e)

V7X_LANES = 128
V7X_SUBLANES = 8
V7X_MXU_WIDTH = 256
V7X_VMEM_LIMIT_BYTES = 56 * 1024 * 1024

Q_TILE = 128
KEY_TILE = 512
TILES_PER_TRIP = 2
SELECT_VARIANTS = 4
WIN_BLOCKS = WINDOW // Q_TILE + 1
AUG_ROWS = 16


def _compiler_params(semantics):
    return pltpu.CompilerParams(dimension_semantics=semantics, vmem_limit_bytes=V7X_VMEM_LIMIT_BYTES)


def _round_up(x, m):
    return -(-x // m) * m


def _rms(xf, gain):
    ms = jnp.mean(xf * xf, axis=-1, keepdims=True)
    return xf * lax.rsqrt(ms + NORM_EPS) * gain


_NT = (((1,), (1,)), ((), ()))


def _norm_body(x_ref, g_ref, h_ref):
    h_ref[...] = _rms(x_ref[...], g_ref[...]).astype(h_ref.dtype)


def rms_norm_cast(x, gain, *, rows=256):
    s, d = x.shape
    return pl.pallas_call(
        _norm_body,
        grid=(s // rows,),
        in_specs=[pl.BlockSpec((rows, d), lambda i: (i, 0)), pl.BlockSpec((1, d), lambda i: (0, 0))],
        out_specs=pl.BlockSpec((rows, d), lambda i: (i, 0)),
        out_shape=jax.ShapeDtypeStruct((s, d), jnp.bfloat16),
        compiler_params=_compiler_params(("arbitrary",)),
        name="rms_norm_cast",
    )(x, gain.reshape(1, d))


def _resnorm_body(x_ref, f_ref, gp_ref, gn_ref, xo_ref, h_ref, *, alpha):
    y = x_ref[...] + alpha * _rms(f_ref[...], gp_ref[...])
    xo_ref[...] = y
    h_ref[...] = _rms(y, gn_ref[...]).astype(h_ref.dtype)


def _resnorm_last_body(x_ref, f_ref, gp_ref, xo_ref, *, alpha):
    xo_ref[...] = x_ref[...] + alpha * _rms(f_ref[...], gp_ref[...])


def residual_norm(x, f, g_post, g_next, alpha, *, rows=256):
    s, d = x.shape
    row_spec = pl.BlockSpec((rows, d), lambda i: (i, 0))
    gain_spec = pl.BlockSpec((1, d), lambda i: (0, 0))
    if g_next is None:
        return pl.pallas_call(
            functools.partial(_resnorm_last_body, alpha=alpha),
            grid=(s // rows,),
            in_specs=[row_spec, row_spec, gain_spec],
            out_specs=row_spec,
            out_shape=jax.ShapeDtypeStruct((s, d), jnp.float32),
            compiler_params=_compiler_params(("arbitrary",)),
            name="residual_norm_last",
        )(x, f, g_post.reshape(1, d)), None
    return pl.pallas_call(
        functools.partial(_resnorm_body, alpha=alpha),
        grid=(s // rows,),
        in_specs=[row_spec, row_spec, gain_spec, gain_spec],
        out_specs=[row_spec, row_spec],
        out_shape=[jax.ShapeDtypeStruct((s, d), jnp.float32), jax.ShapeDtypeStruct((s, d), jnp.bfloat16)],
        compiler_params=_compiler_params(("arbitrary",)),
        name="residual_norm",
    )(x, f, g_post.reshape(1, d), g_next.reshape(1, d))


def _ffn_body(h_ref, wg_ref, wu_ref, wd_ref, o_ref, a_ref, *, n_up, tf):
    j = pl.program_id(1)

    @pl.when(j < n_up)
    def _():
        h = h_ref[...]
        g = jnp.dot(h, wg_ref[...], preferred_element_type=jnp.float32)
        u = jnp.dot(h, wu_ref[...], preferred_element_type=jnp.float32)
        a = (g * jax.nn.sigmoid(g)) * u
        a_ref[:, pl.ds(pl.multiple_of(j * tf, tf), tf)] = a.astype(a_ref.dtype)

    @pl.when(j >= n_up)
    def _():
        o_ref[...] = jnp.dot(a_ref[...], wd_ref[...], preferred_element_type=jnp.float32)


def swiglu_ffn(h, w_gate, w_up, w_down, *, tm=1024, tf=256, tn=256):
    s, d = h.shape
    f = w_gate.shape[1]
    tm = min(tm, s)
    n_up, n_down = f // tf, d // tn
    return pl.pallas_call(
        functools.partial(_ffn_body, n_up=n_up, tf=tf),
        grid=(s // tm, n_up + n_down),
        in_specs=[
            pl.BlockSpec((tm, d), lambda i, j: (i, 0)),
            pl.BlockSpec((d, tf), lambda i, j: (0, jnp.minimum(j, n_up - 1))),
            pl.BlockSpec((d, tf), lambda i, j: (0, jnp.minimum(j, n_up - 1))),
            pl.BlockSpec((f, tn), lambda i, j: (0, jnp.maximum(j - n_up, 0))),
        ],
        out_specs=pl.BlockSpec((tm, tn), lambda i, j: (i, jnp.maximum(j - n_up, 0))),
        out_shape=jax.ShapeDtypeStruct((s, d), jnp.float32),
        scratch_shapes=[pltpu.VMEM((tm, f), jnp.bfloat16)],
        compiler_params=_compiler_params(("arbitrary", "arbitrary")),
        name="swiglu_ffn",
    )(h, w_gate, w_up, w_down)


def _matmul_body(a_ref, b_ref, o_ref):
    o_ref[...] = jnp.dot(a_ref[...], b_ref[...], preferred_element_type=jnp.float32).astype(o_ref.dtype)


def matmul(a, b, out_dtype, *, tm=1024, tn=512, name="matmul"):
    m, k = a.shape
    n = b.shape[1]
    tm, tn = min(tm, m), min(tn, n)
    return pl.pallas_call(
        _matmul_body,
        grid=(m // tm, n // tn),
        in_specs=[pl.BlockSpec((tm, k), lambda i, j: (i, 0)), pl.BlockSpec((k, tn), lambda i, j: (0, j))],
        out_specs=pl.BlockSpec((tm, tn), lambda i, j: (i, j)),
        out_shape=jax.ShapeDtypeStruct((m, n), out_dtype),
        compiler_params=_compiler_params(("arbitrary", "arbitrary")),
        name=name,
    )(a, b)


def _matmul_nt_body(bt_ref, a_ref, o_ref, *, sigmoid, out_scale):
    r = lax.dot_general(bt_ref[...], a_ref[...], _NT, preferred_element_type=jnp.float32)
    if sigmoid:
        r = jax.nn.sigmoid(r)
    if out_scale is not None:
        r = r * out_scale
    o_ref[...] = r.astype(o_ref.dtype)


def matmul_nt(bt, a, out_dtype, *, tm=1024, tn=512, sigmoid=False, out_scale=None, name="matmul_nt"):
    n, k = bt.shape
    m = a.shape[0]
    tm, tn = min(tm, m), min(tn, n)
    return pl.pallas_call(
        functools.partial(_matmul_nt_body, sigmoid=sigmoid, out_scale=out_scale),
        grid=(m // tm, n // tn),
        in_specs=[pl.BlockSpec((tn, k), lambda i, j: (j, 0)), pl.BlockSpec((tm, k), lambda i, j: (i, 0))],
        out_specs=pl.BlockSpec((tn, tm), lambda i, j: (j, i)),
        out_shape=jax.ShapeDtypeStruct((n, m), out_dtype),
        compiler_params=_compiler_params(("arbitrary", "arbitrary")),
        name=name,
    )(bt, a)


POOL_HALO = 16


def _pool_body(h_ref, win_ref, wgrp_ref, scale_ref, y_ref, halo_ref, *, tm):
    grp = pl.program_id(0)
    i = pl.program_id(1)
    u = jnp.dot(h_ref[...], win_ref[...], preferred_element_type=jnp.float32)

    @pl.when(i == 0)
    def _():
        halo_ref[...] = jnp.zeros_like(halo_ref)

    ext = jnp.concatenate([halo_ref[...], u], axis=0)
    halo_ref[...] = u[tm - POOL_HALO:, :]
    s2 = ext + pltpu.roll(ext, 1, 0)
    s4 = s2 + pltpu.roll(s2, 2, 0)
    s8 = s4 + pltpu.roll(s4, 4, 0)
    s16 = s8 + pltpu.roll(s8, 8, 0)
    sw = jnp.where(grp == 0, s2, jnp.where(grp == 1, s4, jnp.where(grp == 2, s8, s16)))[POOL_HALO:, :]
    w = jnp.where(grp == 0, POOL_WINDOWS[0], jnp.where(grp == 1, POOL_WINDOWS[1],
                  jnp.where(grp == 2, POOL_WINDOWS[2], POOL_WINDOWS[3])))
    t = i * tm + lax.broadcasted_iota(jnp.int32, (tm, 1), 0)
    count = jnp.minimum(t + 1, w).astype(jnp.float32)
    d = sw / count - u
    y = jnp.dot(d.astype(jnp.bfloat16), wgrp_ref[0], preferred_element_type=jnp.float32)
    y_ref[...] = (y * scale_ref[...]).astype(y_ref.dtype)


def pool_front(h, w_in, w_group, scale, *, tm=512):
    s, d = h.shape
    n_groups, gw, _ = w_group.shape
    tm = min(tm, s)
    return pl.pallas_call(
        functools.partial(_pool_body, tm=tm),
        grid=(n_groups, s // tm),
        in_specs=[
            pl.BlockSpec((tm, d), lambda g, i: (i, 0)),
            pl.BlockSpec((d, gw), lambda g, i: (0, g)),
            pl.BlockSpec((1, gw, gw), lambda g, i: (g, 0, 0)),
            pl.BlockSpec((1, gw), lambda g, i: (0, g)),
        ],
        out_specs=pl.BlockSpec((tm, gw), lambda g, i: (i, g)),
        out_shape=jax.ShapeDtypeStruct((s, d), jnp.bfloat16),
        scratch_shapes=[pltpu.VMEM((POOL_HALO, gw), jnp.float32)],
        compiler_params=_compiler_params(("arbitrary", "arbitrary")),
        name="pool_front",
    )(h, w_in, w_group, scale.reshape(1, d))


def _bucket_thresholds():
    thr = list(range(1, MAX_EXACT + 1))
    for k in range(1, N_BUCKETS - MAX_EXACT):
        thr.append(math.isqrt(256 * 2 ** k - 1) + 1)
    return tuple(thr)


BUCKET_THRESHOLDS = _bucket_thresholds()
LAST_BUCKET_DIST = BUCKET_THRESHOLDS[-1]
STRIP_ROW_BLOCK = 256


def _strip_body(tab_ref, o_ref, *, rows, row_step, const, max_dist):
    h = pl.program_id(0)
    r = pl.program_id(1)
    a = r * rows + lax.broadcasted_iota(jnp.int32, (rows, V7X_LANES), 0)
    l = lax.broadcasted_iota(jnp.int32, (rows, V7X_LANES), 1)
    dist = l + const - row_step * a
    v = jnp.full((rows, V7X_LANES), tab_ref[0, h], jnp.float32)
    for b, thr in enumerate(BUCKET_THRESHOLDS, start=1):
        v = jnp.where(dist >= thr, tab_ref[b, h], v)
    ok = dist >= 0
    if max_dist is not None:
        ok = jnp.logical_and(ok, dist < max_dist)
    o_ref[0] = jnp.where(ok, v * LOG2_E, NEG_INF)


def bias_strip(rel_table, min_rows, *, row_step, const, max_dist):
    n_heads = rel_table.shape[1]
    rows = STRIP_ROW_BLOCK
    n_rows = _round_up(min_rows, rows)
    return pl.pallas_call(
        functools.partial(_strip_body, rows=rows, row_step=row_step, const=const, max_dist=max_dist),
        grid=(n_heads, n_rows // rows),
        in_specs=[pl.BlockSpec(memory_space=pltpu.SMEM)],
        out_specs=pl.BlockSpec((1, rows, V7X_LANES), lambda h, r: (h, r, 0)),
        out_shape=jax.ShapeDtypeStruct((n_heads, n_rows, V7X_LANES), jnp.float32),
        compiler_params=_compiler_params(("arbitrary", "arbitrary")),
        name="bias_strip",
    )(rel_table)


SEL_STRIP_CONST = _round_up(LAST_BUCKET_DIST + V7X_SUBLANES - 1, V7X_SUBLANES)
SEL_STRIP_ROWS = SEL_STRIP_CONST + KEY_TILE
WIN_STRIP_CONST = _round_up(WINDOW + V7X_SUBLANES - 1, V7X_SUBLANES)
WIN_STRIP_ROWS = WIN_STRIP_CONST + Q_TILE
CMP_DIST_OFFSET = CMP_BLOCK - 1
CMP_STRIP_CONST = _round_up(LAST_BUCKET_DIST + CMP_STRIDE * (V7X_SUBLANES - 1) + CMP_DIST_OFFSET,
                            CMP_STRIDE * V7X_SUBLANES) - CMP_DIST_OFFSET
CMP_STRIP_ROW0 = (CMP_STRIP_CONST + CMP_DIST_OFFSET) // CMP_STRIDE
CMP_STRIP_HI = _round_up((V7X_LANES + CMP_STRIP_CONST) // CMP_STRIDE + 1, V7X_SUBLANES)
CMP_STRIP_ROWS = CMP_STRIP_HI + V7X_SUBLANES


def nsa_strips(rel_bias):
    sel_strip = bias_strip(rel_bias, SEL_STRIP_ROWS, row_step=1, const=SEL_STRIP_CONST, max_dist=None)
    win_strip = bias_strip(rel_bias, WIN_STRIP_ROWS, row_step=1, const=WIN_STRIP_CONST, max_dist=WINDOW)
    cmp_strip = bias_strip(rel_bias, CMP_STRIP_ROWS, row_step=CMP_STRIDE, const=CMP_STRIP_CONST, max_dist=None)
    return sel_strip, win_strip, cmp_strip


def _bias_columns(strip_ref, heads, offsets):
    cols = [jnp.concatenate([strip_ref[h, pl.ds(e, V7X_SUBLANES), :] for e in offsets], axis=0) for h in heads]
    return jnp.concatenate(cols, axis=1) if len(cols) > 1 else cols[0]


def _compress_body(x_ref, pos_ref, w1a_ref, w1b_ref, w2_ref, o_ref, acca_ref, accb_ref, *, n_chunks):
    l = pl.program_id(1)

    @pl.when(l == 0)
    def _():
        acca_ref[...] = jnp.zeros_like(acca_ref)
        accb_ref[...] = jnp.zeros_like(accb_ref)

    x = x_ref[...]
    pa = pos_ref[0, pl.ds(l, 1), :]
    pb = pos_ref[0, pl.ds(l + CMP_STRIDE, 1), :]
    acca_ref[...] += jnp.dot((x + pa).astype(jnp.bfloat16), w1a_ref[0], preferred_element_type=jnp.float32)
    accb_ref[...] += jnp.dot((x + pb).astype(jnp.bfloat16), w1b_ref[0], preferred_element_type=jnp.float32)

    @pl.when(l == CMP_STRIDE - 1)
    def _():
        pre = acca_ref[...] + pltpu.roll(accb_ref[...], n_chunks - 1, 0)
        act = jax.nn.gelu(pre)
        o_ref[0] = jnp.dot(act.astype(jnp.bfloat16), w2_ref[0],
                           preferred_element_type=jnp.float32).astype(o_ref.dtype)


def compress_kv(kv_cmp, pos, w1, w2):
    s = kv_cmp.shape[0]
    dh = HEAD_DIM
    n_chunks = s // CMP_STRIDE
    n_kvg = 2 * N_KV_GROUPS
    x = kv_cmp.reshape(n_chunks, CMP_STRIDE * n_kvg * dh)
    w1r = w1.reshape(2, CMP_BLOCK, dh, dh)
    return pl.pallas_call(
        functools.partial(_compress_body, n_chunks=n_chunks),
        grid=(n_kvg, CMP_STRIDE),
        in_specs=[
            pl.BlockSpec((n_chunks, dh), lambda c, l: (0, l * n_kvg + c)),
            pl.BlockSpec((1, CMP_BLOCK, dh), lambda c, l: (c // N_KV_GROUPS, 0, 0)),
            pl.BlockSpec((1, None, dh, dh), lambda c, l: (c // N_KV_GROUPS, l, 0, 0)),
            pl.BlockSpec((1, None, dh, dh), lambda c, l: (c // N_KV_GROUPS, l + CMP_STRIDE, 0, 0)),
            pl.BlockSpec((1, dh, dh), lambda c, l: (c // N_KV_GROUPS, 0, 0)),
        ],
        out_specs=pl.BlockSpec((1, n_chunks, dh), lambda c, l: (c, 0, 0)),
        out_shape=jax.ShapeDtypeStruct((n_kvg, n_chunks, dh), jnp.bfloat16),
        scratch_shapes=[pltpu.VMEM((n_chunks, dh), jnp.float32), pltpu.VMEM((n_chunks, dh), jnp.float32)],
        compiler_params=_compiler_params(("arbitrary", "arbitrary")),
        name="compress_kv",
    )(x, pos, w1r, w1r, w2)


def _online_softmax_tile(s, vt, m, acc_ref, skip=None):
    m_new = jnp.maximum(m, jnp.max(s, axis=0, keepdims=True))
    alpha = jnp.exp2(m - m_new)
    m_sub = m_new
    if skip is not None:
        m_new = jnp.where(skip, m, m_new)
        alpha = jnp.where(skip, 1.0, alpha)
        m_sub = jnp.where(skip, -NEG_INF, m_new)
    p = jnp.exp2(s - m_sub).astype(jnp.bfloat16)
    acc_ref[...] = alpha * acc_ref[...] + jnp.dot(vt, p, preferred_element_type=jnp.float32)
    return m_new


def _ones_rows(keys):
    row = lax.broadcasted_iota(jnp.int32, (AUG_ROWS, keys), 0)
    return jnp.where(row == 0, 1.0, 0.0).astype(jnp.bfloat16)


def _gate_row(gt_ref, first_row, heads):
    rows = gt_ref[pl.ds(pl.multiple_of(first_row, V7X_SUBLANES), V7X_SUBLANES), :]
    return jnp.concatenate([rows[h:h + 1, :] for h in range(heads)], axis=1)


def _nsa_body(q_ref, kc_ref, vct_ref, ks_ref, vst_ref, *rest, heads, n_cmp, n_sel):
    kw_refs, vwt_refs = rest[:WIN_BLOCKS], rest[WIN_BLOCKS:2 * WIN_BLOCKS]
    (cstrip_ref, sstrip_ref, wstrip_ref, gt_ref, poolt_ref, o_ref,
     acc_ref, out_ref, p_ref, selm_ref, sa_ref, sb_ref) = rest[2 * WIN_BLOCKS:]
    g = pl.program_id(0)
    qb = pl.program_id(1)
    t0 = qb * Q_TILE
    dh = HEAD_DIM
    lanes = heads * Q_TILE
    heads_per_chunk = V7X_MXU_WIDTH // Q_TILE
    qt = jnp.concatenate([q_ref[h * dh:(h + 1) * dh, :] for h in range(heads)], axis=1)
    gate_base = g * heads

    def compressed_and_select(n_blocks):
        n_rows = n_blocks * (SEL_BLOCK // CMP_STRIDE)
        kc = kc_ref[0, :n_rows, :]
        cmp_offsets = [pl.multiple_of(jnp.clip(CMP_STRIP_ROW0 - qb * (Q_TILE // CMP_STRIDE) + m0, 0, CMP_STRIP_HI),
                                      V7X_SUBLANES) for m0 in range(0, n_rows, V7X_SUBLANES)]
        imp = jnp.zeros((n_rows, Q_TILE), jnp.float32)
        for c in range(heads // heads_per_chunk):
            hs = range(c * heads_per_chunk, (c + 1) * heads_per_chunk)
            lo, hi = c * V7X_MXU_WIDTH, (c + 1) * V7X_MXU_WIDTH
            s = jnp.dot(kc, qt[:, lo:hi], preferred_element_type=jnp.float32)
            s = s + _bias_columns(cstrip_ref, hs, cmp_offsets)
            m = jnp.max(s, axis=0, keepdims=True)
            e = jnp.exp2(s - m)
            l = jnp.maximum(jnp.sum(e, axis=0, keepdims=True), 1e-30)
            p = e * jnp.where(m > 0.5 * NEG_INF, 1.0 / l, 0.0)
            for k in range(heads_per_chunk):
                imp = imp + p[:, k * Q_TILE:(k + 1) * Q_TILE]
            p_ref[:n_rows, lo:hi] = p.astype(p_ref.dtype)
        gate_c = _gate_row(gt_ref, gate_base, heads)
        out_ref[...] = jnp.dot(vct_ref[0, :, :n_rows], p_ref[:n_rows, :], preferred_element_type=jnp.float32) * gate_c

        poolt = poolt_ref[:n_blocks, :n_rows]
        hi_t = imp.astype(jnp.bfloat16)
        r1 = imp - hi_t.astype(jnp.float32)
        mid_t = r1.astype(jnp.bfloat16)
        lo_t = (r1 - mid_t.astype(jnp.float32)).astype(jnp.bfloat16)
        imp_sel = (jnp.dot(poolt, hi_t, preferred_element_type=jnp.float32)
                   + jnp.dot(poolt, mid_t, preferred_element_type=jnp.float32)
                   + jnp.dot(poolt, lo_t, preferred_element_type=jnp.float32))

        j = lax.broadcasted_iota(jnp.int32, (n_blocks, Q_TILE), 0)
        t = t0 + lax.broadcasted_iota(jnp.int32, (n_blocks, Q_TILE), 1)
        cur = jnp.right_shift(t, SEL_BLOCK_LOG2)
        forced = jnp.logical_or(j == 0, jnp.logical_or(j == cur, j == cur - 1))
        valid = j * SEL_BLOCK <= t
        n_forced = 3
        score = jnp.where(jnp.logical_and(valid, jnp.logical_not(forced)), imp_sel, NEG_INF)
        picked = jnp.where(forced, 1.0, 0.0)
        jf = j.astype(jnp.float32)
        for _ in range(min(N_SELECT, n_sel) - n_forced):
            best = jnp.max(score, axis=0, keepdims=True)
            first = jnp.min(jnp.where(score == best, jf, float(n_blocks)), axis=0, keepdims=True)
            hit = jf == first
            picked = jnp.where(hit, 1.0, picked)
            score = jnp.where(hit, BELOW_NEG_INF, score)
        selm_ref[:n_blocks, :] = jnp.concatenate([picked - 1.0] * heads, axis=1)
        if n_blocks < n_sel:
            selm_ref[n_blocks:, :] = jnp.full((n_sel - n_blocks, lanes), -1.0, jnp.float32)

    blocks_step = n_sel // SELECT_VARIANTS
    variant = (t0 + Q_TILE - 1) // (SEL_BLOCK * blocks_step)
    for v in range(SELECT_VARIANTS):
        pl.when(variant == v)(functools.partial(compressed_and_select, (v + 1) * blocks_step))

    all_heads = range(heads)
    win_offsets = []
    for back in range(WIN_BLOCKS):
        for i0 in range(0, Q_TILE, V7X_SUBLANES):
            e = WIN_STRIP_CONST - back * Q_TILE + i0
            win_offsets.append(e if back == 0 else pl.multiple_of(jnp.where(back > qb, 0, e), V7X_SUBLANES))
    k_w = jnp.concatenate([r[...] for r in kw_refs], axis=0)
    vt_w = jnp.concatenate([jnp.concatenate([r[...] for r in vwt_refs], axis=1),
                            _ones_rows(WIN_BLOCKS * Q_TILE)], axis=0)
    s_w = jnp.dot(k_w, qt, preferred_element_type=jnp.float32) + _bias_columns(wstrip_ref, all_heads, win_offsets)
    p_w = jnp.exp2(s_w - jnp.max(s_w, axis=0, keepdims=True)).astype(jnp.bfloat16)
    acc_w = jnp.dot(vt_w, p_w, preferred_element_type=jnp.float32)
    gate_w = _gate_row(gt_ref, 2 * N_KV_GROUPS * heads + gate_base, heads)
    out_ref[...] += acc_w[:dh, :] * (gate_w / acc_w[dh:dh + 1, :])

    blocks_per_tile = KEY_TILE // SEL_BLOCK
    key_blk = jnp.right_shift(lax.broadcasted_iota(jnp.int32, (KEY_TILE, V7X_LANES), 0), SEL_BLOCK_LOG2)
    blk_col = lax.broadcasted_iota(jnp.int32, (KEY_TILE, V7X_LANES), 1)
    key_mask_cols = jnp.where(key_blk == blk_col, -NEG_INF, 0.0).astype(jnp.bfloat16)
    zero_rows = jnp.zeros((V7X_LANES - AUG_ROWS, lanes), jnp.bfloat16)
    sel_ones = _ones_rows(KEY_TILE)

    last_tile = (t0 + Q_TILE - 1) // KEY_TILE

    def scores(kt):
        ktc = jnp.minimum(kt, last_tile)
        start = pl.multiple_of(ktc * KEY_TILE, KEY_TILE)
        k_aug = jnp.concatenate([ks_ref[pl.ds(start, KEY_TILE), :], key_mask_cols], axis=1)
        sel_rows = selm_ref[pl.ds(pl.multiple_of(ktc * blocks_per_tile, blocks_per_tile), blocks_per_tile), :]
        sel_rows = jnp.concatenate([sel_rows, jnp.zeros((AUG_ROWS - blocks_per_tile, lanes), jnp.float32)], axis=0)
        qt_aug = jnp.concatenate([qt, sel_rows.astype(jnp.bfloat16), zero_rows], axis=0)
        row0 = SEL_STRIP_CONST - (t0 - ktc * KEY_TILE)
        offsets = [pl.multiple_of(jnp.maximum(row0 + i0, 0), V7X_SUBLANES) for i0 in range(0, KEY_TILE, V7X_SUBLANES)]
        s = jnp.dot(k_aug, qt_aug, preferred_element_type=jnp.float32)
        return s + _bias_columns(sstrip_ref, all_heads, offsets)

    def values(kt):
        start = pl.multiple_of(jnp.minimum(kt, last_tile) * KEY_TILE, KEY_TILE)
        return jnp.concatenate([vst_ref[:, pl.ds(start, KEY_TILE)], sel_ones], axis=0)

    def sel_step(i, m):
        kt = TILES_PER_TRIP * i
        for k in range(0, TILES_PER_TRIP, 2):
            sb_ref[...] = scores(kt + k + 1)
            m = _online_softmax_tile(sa_ref[...], values(kt + k), m, acc_ref,
                                     skip=None if k == 0 else kt + k > last_tile)
            sa_ref[...] = scores(kt + k + 2)
            m = _online_softmax_tile(sb_ref[...], values(kt + k + 1), m, acc_ref, skip=kt + k + 1 > last_tile)
        return m

    acc_ref[...] = jnp.zeros_like(acc_ref)
    m_init = jnp.full((1, lanes), NEG_INF, jnp.float32)
    sa_ref[...] = scores(0)
    lax.fori_loop(0, last_tile // TILES_PER_TRIP + 1, sel_step, m_init)
    gate_s = _gate_row(gt_ref, N_KV_GROUPS * heads + gate_base, heads)
    out_t = out_ref[...] + acc_ref[:dh, :] * (gate_s / acc_ref[dh:dh + 1, :])
    for h in range(heads):
        o_ref[:, h * dh:(h + 1) * dh] = out_t[:, h * Q_TILE:(h + 1) * Q_TILE].T.astype(o_ref.dtype)


def _pool_matrix_t(n_sel, n_cmp):
    r = SEL_BLOCK // CMP_STRIDE
    c = CMP_BLOCK // CMP_STRIDE
    j = jnp.arange(n_sel)[:, None]
    m = jnp.arange(n_cmp)[None, :]
    lo = r * j - (c - 1)
    return jnp.logical_and(m >= lo, m <= lo + r + c - 2).astype(jnp.bfloat16)


def nsa_attention(q_t, k_c, vt_c, k_sw, vt_sw, gates_t, strips):
    dq, s = q_t.shape
    dh = HEAD_DIM
    heads = dq // dh // N_KV_GROUPS
    assert heads == V7X_SUBLANES, "gate rows of one group must fill one sublane tile"
    assert KEY_TILE // SEL_BLOCK == V7X_SUBLANES, "one key tile's selection rows must fill one sublane tile"
    gw = heads * dh
    n_cmp = k_c.shape[1]
    n_sel = s // SEL_BLOCK
    sel_strip, win_strip, cmp_strip = strips
    once_per_group = pl.Buffered(1)

    def kw_spec(back):
        return pl.BlockSpec((Q_TILE, dh), lambda g, qb: (jnp.maximum(qb - back, 0), N_KV_GROUPS + g))

    def vwt_spec(back):
        return pl.BlockSpec((dh, Q_TILE), lambda g, qb: (N_KV_GROUPS + g, jnp.maximum(qb - back, 0)))

    def strip_spec(strip):
        return pl.BlockSpec((heads, strip.shape[1], V7X_LANES), lambda g, qb: (g, 0, 0), pipeline_mode=once_per_group)

    return pl.pallas_call(
        functools.partial(_nsa_body, heads=heads, n_cmp=n_cmp, n_sel=n_sel),
        grid=(N_KV_GROUPS, s // Q_TILE),
        in_specs=[
            pl.BlockSpec((gw, Q_TILE), lambda g, qb: (g, qb)),
            pl.BlockSpec((1, n_cmp, dh), lambda g, qb: (g, 0, 0)),
            pl.BlockSpec((1, dh, n_cmp), lambda g, qb: (g, 0, 0)),
            pl.BlockSpec((s, dh), lambda g, qb: (0, g), pipeline_mode=once_per_group),
            pl.BlockSpec((dh, s), lambda g, qb: (g, 0), pipeline_mode=once_per_group),
            *[kw_spec(back) for back in range(WIN_BLOCKS)], *[vwt_spec(back) for back in range(WIN_BLOCKS)],
            strip_spec(cmp_strip), strip_spec(sel_strip), strip_spec(win_strip),
            pl.BlockSpec((gates_t.shape[0], Q_TILE), lambda g, qb: (0, qb)),
            pl.BlockSpec((n_sel, n_cmp), lambda g, qb: (0, 0)),
        ],
        out_specs=pl.BlockSpec((Q_TILE, gw), lambda g, qb: (qb, g)),
        out_shape=jax.ShapeDtypeStruct((s, dq), jnp.bfloat16),
        scratch_shapes=[pltpu.VMEM((dh + AUG_ROWS, heads * Q_TILE), jnp.float32),
                        pltpu.VMEM((dh, heads * Q_TILE), jnp.float32),
                        pltpu.VMEM((n_cmp, heads * Q_TILE), jnp.bfloat16),
                        pltpu.VMEM((n_sel, heads * Q_TILE), jnp.float32),
                        pltpu.VMEM((KEY_TILE, heads * Q_TILE), jnp.float32),
                        pltpu.VMEM((KEY_TILE, heads * Q_TILE), jnp.float32)],
        compiler_params=_compiler_params(("arbitrary", "arbitrary")),
        name="nsa_attention",
    )(q_t, k_c, vt_c, k_sw, vt_sw, *[k_sw] * WIN_BLOCKS, *[vt_sw] * WIN_BLOCKS, cmp_strip, sel_strip, win_strip, gates_t,
      _pool_matrix_t(n_sel, n_cmp))


def _bf16(w):
    return w.astype(jnp.bfloat16)


def nsa_mixer(h, w_in, cmp_pos, cmp_w1, cmp_w2, w_out, strips):
    dh, g = HEAD_DIM, N_KV_GROUPS
    q_end = w_out.shape[0]
    half = g * dh
    kv_end = q_end + N_BRANCHES * 2 * half
    n_gates = w_in.shape[1] - kv_end
    scale2 = dh ** -0.5 * LOG2_E

    def cols(branch, is_v):
        lo = q_end + branch * 2 * half + (half if is_v else 0)
        return w_in[:, lo:lo + half]

    q_t = matmul_nt(_bf16(w_in[:, :q_end].T), h, jnp.bfloat16, out_scale=scale2, name="nsa_q_proj")
    kv_cmp = matmul(h, _bf16(w_in[:, q_end:q_end + 2 * half]), jnp.float32, name="nsa_kv_cmp_proj")
    k_sw = matmul(h, _bf16(jnp.concatenate([cols(1, False), cols(2, False)], axis=1)), jnp.bfloat16,
                  name="nsa_k_proj")
    vt_sw = matmul_nt(_bf16(jnp.concatenate([cols(1, True), cols(2, True)], axis=1).T), h, jnp.bfloat16,
                      name="nsa_vt_proj")
    w_gate_t = jnp.pad(_bf16(w_in[:, kv_end:].T), ((0, V7X_LANES - n_gates), (0, 0)))
    gates_t = matmul_nt(w_gate_t, h, jnp.float32, sigmoid=True, name="nsa_gate_proj")

    kv_c = compress_kv(kv_cmp, cmp_pos, _bf16(cmp_w1), _bf16(cmp_w2))
    k_c = kv_c[:g]
    vt_c = jnp.swapaxes(kv_c[g:], 1, 2)
    o = nsa_attention(q_t, k_c, vt_c, k_sw, vt_sw, gates_t, strips)
    return matmul(o, _bf16(w_out), jnp.float32, name="nsa_out_proj")


def pool_mixer(h, w_in, w_group, scale, w_out):
    y = pool_front(h, _bf16(w_in), _bf16(w_group), scale)
    return matmul(y, _bf16(w_out), jnp.float32, name="pool_out_proj")


def kernel(x, norm_gains, ffn_w_gate, ffn_w_up, ffn_w_down, pool_w_in, pool_w_group, pool_scale, pool_w_out,
           nsa_w_in, nsa_cmp_pos, nsa_cmp_w1, nsa_cmp_w2, nsa_w_out, rel_bias):
    b, s, d = x.shape
    depth = norm_gains.shape[0]
    outs = []
    strips = nsa_strips(rel_bias) if depth > 1 else None
    for bi in range(b):
        xs = x[bi]
        h = rms_norm_cast(xs, norm_gains[0, 0])
        for i in range(depth):
            gains = norm_gains[i]
            f1 = swiglu_ffn(h, _bf16(ffn_w_gate[i, 0]), _bf16(ffn_w_up[i, 0]), _bf16(ffn_w_down[i, 0]))
            xs, h = residual_norm(xs, f1, gains[1], gains[2], 0.5)
            li = i // 2
            if i % 2 == 0:
                m = pool_mixer(h, pool_w_in[li], pool_w_group[li], pool_scale[li], pool_w_out[li])
            else:
                m = nsa_mixer(h, nsa_w_in[li], nsa_cmp_pos[li], nsa_cmp_w1[li], nsa_cmp_w2[li], nsa_w_out[li],
                              strips)
            xs, h = residual_norm(xs, m, gains[3], gains[4], 1.0)
            f2 = swiglu_ffn(h, _bf16(ffn_w_gate[i, 1]), _bf16(ffn_w_up[i, 1]), _bf16(ffn_w_down[i, 1]))
            g_next = norm_gains[i + 1, 0] if i + 1 < depth else None
            xs, h = residual_norm(xs, f2, gains[5], g_next, 0.5)
        outs.append(xs)
    return jnp.stack(outs, axis=0)
```

```python
import functools
import math

import jax
import jax.numpy as jnp
from jax import lax
from jax.experimental import pallas as pl
from jax.experimental.pallas import tpu as pltpu

NORM_EPS = 1e-6
POOL_WINDOWS = (2, 4, 8, 16)
HEAD_DIM = 128
N_KV_GROUPS = 4
N_BRANCHES = 3
CMP_BLOCK = 32
CMP_STRIDE = 16
SEL_BLOCK = 64
SEL_BLOCK_LOG2 = 6
N_SELECT = 16
WINDOW = 512
N_BUCKETS = 32
MAX_EXACT = N_BUCKETS // 2
NEG_INF = -1e30
BELOW_NEG_INF = -3e38
LOG2_E = math.log2(math.e)

V7X_LANES = 128
V7X_SUBLANES = 8
V7X_MXU_WIDTH = 256
V7X_VMEM_LIMIT_BYTES = 56 * 1024 * 1024

Q_TILE = 128
KEY_TILE = 512
TILES_PER_TRIP = 2
SELECT_VARIANTS = 4
WIN_BLOCKS = WINDOW // Q_TILE + 1
AUG_ROWS = 16


def _compiler_params(semantics):
    return pltpu.CompilerParams(dimension_semantics=semantics, vmem_limit_bytes=V7X_VMEM_LIMIT_BYTES)


def _round_up(x, m):
    return -(-x // m) * m


def _rms(xf, gain):
    ms = jnp.mean(xf * xf, axis=-1, keepdims=True)
    return xf * lax.rsqrt(ms + NORM_EPS) * gain


_NT = (((1,), (1,)), ((), ()))


def _norm_body(x_ref, g_ref, h_ref):
    h_ref[...] = _rms(x_ref[...], g_ref[...]).astype(h_ref.dtype)


def rms_norm_cast(x, gain, *, rows=256):
    s, d = x.shape
    return pl.pallas_call(
        _norm_body,
        grid=(s // rows,),
        in_specs=[pl.BlockSpec((rows, d), lambda i: (i, 0)), pl.BlockSpec((1, d), lambda i: (0, 0))],
        out_specs=pl.BlockSpec((rows, d), lambda i: (i, 0)),
        out_shape=jax.ShapeDtypeStruct((s, d), jnp.bfloat16),
        compiler_params=_compiler_params(("arbitrary",)),
        name="rms_norm_cast",
    )(x, gain.reshape(1, d))


def _resnorm_body(x_ref, f_ref, gp_ref, gn_ref, xo_ref, h_ref, *, alpha):
    y = x_ref[...] + alpha * _rms(f_ref[...], gp_ref[...])
    xo_ref[...] = y
    h_ref[...] = _rms(y, gn_ref[...]).astype(h_ref.dtype)


def _resnorm_last_body(x_ref, f_ref, gp_ref, xo_ref, *, alpha):
    xo_ref[...] = x_ref[...] + alpha * _rms(f_ref[...], gp_ref[...])


def residual_norm(x, f, g_post, g_next, alpha, *, rows=256):
    s, d = x.shape
    row_spec = pl.BlockSpec((rows, d), lambda i: (i, 0))
    gain_spec = pl.BlockSpec((1, d), lambda i: (0, 0))
    if g_next is None:
        return pl.pallas_call(
            functools.partial(_resnorm_last_body, alpha=alpha),
            grid=(s // rows,),
            in_specs=[row_spec, row_spec, gain_spec],
            out_specs=row_spec,
            out_shape=jax.ShapeDtypeStruct((s, d), jnp.float32),
            compiler_params=_compiler_params(("arbitrary",)),
            name="residual_norm_last",
        )(x, f, g_post.reshape(1, d)), None
    return pl.pallas_call(
        functools.partial(_resnorm_body, alpha=alpha),
        grid=(s // rows,),
        in_specs=[row_spec, row_spec, gain_spec, gain_spec],
        out_specs=[row_spec, row_spec],
        out_shape=[jax.ShapeDtypeStruct((s, d), jnp.float32), jax.ShapeDtypeStruct((s, d), jnp.bfloat16)],
        compiler_params=_compiler_params(("arbitrary",)),
        name="residual_norm",
    )(x, f, g_post.reshape(1, d), g_next.reshape(1, d))


def _ffn_body(h_ref, wg_ref, wu_ref, wd_ref, o_ref, a_ref, *, n_up, tf):
    j = pl.program_id(1)

    @pl.when(j < n_up)
    def _():
        h = h_ref[...]
        g = jnp.dot(h, wg_ref[...], preferred_element_type=jnp.float32)
        u = jnp.dot(h, wu_ref[...], preferred_element_type=jnp.float32)
        a = (g * jax.nn.sigmoid(g)) * u
        a_ref[:, pl.ds(pl.multiple_of(j * tf, tf), tf)] = a.astype(a_ref.dtype)

    @pl.when(j >= n_up)
    def _():
        o_ref[...] = jnp.dot(a_ref[...], wd_ref[...], preferred_element_type=jnp.float32)


def swiglu_ffn(h, w_gate, w_up, w_down, index=(), *, tm=1024, tf=256, tn=256):
    s, d = h.shape
    f = w_gate.shape[-1]
    tm = min(tm, s)
    n_up, n_down = f // tf, d // tn
    lead = (None,) * len(index)
    return pl.pallas_call(
        functools.partial(_ffn_body, n_up=n_up, tf=tf),
        grid=(s // tm, n_up + n_down),
        in_specs=[
            pl.BlockSpec((tm, d), lambda i, j: (i, 0)),
            pl.BlockSpec(lead + (d, tf), lambda i, j: (*index, 0, jnp.minimum(j, n_up - 1))),
            pl.BlockSpec(lead + (d, tf), lambda i, j: (*index, 0, jnp.minimum(j, n_up - 1))),
            pl.BlockSpec(lead + (f, tn), lambda i, j: (*index, 0, jnp.maximum(j - n_up, 0))),
        ],
        out_specs=pl.BlockSpec((tm, tn), lambda i, j: (i, jnp.maximum(j - n_up, 0))),
        out_shape=jax.ShapeDtypeStruct((s, d), jnp.float32),
        scratch_shapes=[pltpu.VMEM((tm, f), jnp.bfloat16)],
        compiler_params=_compiler_params(("arbitrary", "arbitrary")),
        name="swiglu_ffn",
    )(h, w_gate, w_up, w_down)


def _matmul_body(a_ref, b_ref, o_ref):
    o_ref[...] = jnp.dot(a_ref[...], b_ref[...], preferred_element_type=jnp.float32).astype(o_ref.dtype)


def matmul(a, b, out_dtype, index=(), *, tm=1024, tn=512, name="matmul"):
    m, k = a.shape
    n = b.shape[-1]
    tm, tn = min(tm, m), min(tn, n)
    lead = (None,) * len(index)
    return pl.pallas_call(
        _matmul_body,
        grid=(m // tm, n // tn),
        in_specs=[pl.BlockSpec((tm, k), lambda i, j: (i, 0)),
                  pl.BlockSpec(lead + (k, tn), lambda i, j: (*index, 0, j))],
        out_specs=pl.BlockSpec((tm, tn), lambda i, j: (i, j)),
        out_shape=jax.ShapeDtypeStruct((m, n), out_dtype),
        compiler_params=_compiler_params(("arbitrary", "arbitrary")),
        name=name,
    )(a, b)


def _matmul_nt_body(bt_ref, a_ref, o_ref, *, sigmoid, out_scale):
    r = lax.dot_general(bt_ref[...], a_ref[...], _NT, preferred_element_type=jnp.float32)
    if sigmoid:
        r = jax.nn.sigmoid(r)
    if out_scale is not None:
        r = r * out_scale
    o_ref[...] = r.astype(o_ref.dtype)


def matmul_nt(bt, a, out_dtype, *, tm=1024, tn=512, sigmoid=False, out_scale=None, name="matmul_nt"):
    n, k = bt.shape
    m = a.shape[0]
    tm, tn = min(tm, m), min(tn, n)
    return pl.pallas_call(
        functools.partial(_matmul_nt_body, sigmoid=sigmoid, out_scale=out_scale),
        grid=(m // tm, n // tn),
        in_specs=[pl.BlockSpec((tn, k), lambda i, j: (j, 0)), pl.BlockSpec((tm, k), lambda i, j: (i, 0))],
        out_specs=pl.BlockSpec((tn, tm), lambda i, j: (j, i)),
        out_shape=jax.ShapeDtypeStruct((n, m), out_dtype),
        compiler_params=_compiler_params(("arbitrary", "arbitrary")),
        name=name,
    )(bt, a)


POOL_HALO = 16


def _pool_body(h_ref, win_ref, wgrp_ref, scale_ref, y_ref, halo_ref, *, tm):
    grp = pl.program_id(0)
    i = pl.program_id(1)
    u = jnp.dot(h_ref[...], win_ref[...], preferred_element_type=jnp.float32)

    @pl.when(i == 0)
    def _():
        halo_ref[...] = jnp.zeros_like(halo_ref)

    ext = jnp.concatenate([halo_ref[...], u], axis=0)
    halo_ref[...] = u[tm - POOL_HALO:, :]
    s2 = ext + pltpu.roll(ext, 1, 0)
    s4 = s2 + pltpu.roll(s2, 2, 0)
    s8 = s4 + pltpu.roll(s4, 4, 0)
    s16 = s8 + pltpu.roll(s8, 8, 0)
    sw = jnp.where(grp == 0, s2, jnp.where(grp == 1, s4, jnp.where(grp == 2, s8, s16)))[POOL_HALO:, :]
    w = jnp.where(grp == 0, POOL_WINDOWS[0], jnp.where(grp == 1, POOL_WINDOWS[1],
                  jnp.where(grp == 2, POOL_WINDOWS[2], POOL_WINDOWS[3])))
    t = i * tm + lax.broadcasted_iota(jnp.int32, (tm, 1), 0)
    count = jnp.minimum(t + 1, w).astype(jnp.float32)
    d = sw / count - u
    y = jnp.dot(d.astype(jnp.bfloat16), wgrp_ref[0], preferred_element_type=jnp.float32)
    y_ref[...] = (y * scale_ref[...]).astype(y_ref.dtype)


def pool_front(h, w_in, w_group, scale, layer, *, tm=512):
    s, d = h.shape
    _, n_groups, gw, _ = w_group.shape
    tm = min(tm, s)
    return pl.pallas_call(
        functools.partial(_pool_body, tm=tm),
        grid=(n_groups, s // tm),
        in_specs=[
            pl.BlockSpec((tm, d), lambda g, i: (i, 0)),
            pl.BlockSpec((None, d, gw), lambda g, i: (layer, 0, g)),
            pl.BlockSpec((None, 1, gw, gw), lambda g, i: (layer, g, 0, 0)),
            pl.BlockSpec((1, gw), lambda g, i: (0, g)),
        ],
        out_specs=pl.BlockSpec((tm, gw), lambda g, i: (i, g)),
        out_shape=jax.ShapeDtypeStruct((s, d), jnp.bfloat16),
        scratch_shapes=[pltpu.VMEM((POOL_HALO, gw), jnp.float32)],
        compiler_params=_compiler_params(("arbitrary", "arbitrary")),
        name="pool_front",
    )(h, w_in, w_group, scale.reshape(1, d))


def _bucket_thresholds():
    thr = list(range(1, MAX_EXACT + 1))
    for k in range(1, N_BUCKETS - MAX_EXACT):
        thr.append(math.isqrt(256 * 2 ** k - 1) + 1)
    return tuple(thr)


BUCKET_THRESHOLDS = _bucket_thresholds()
LAST_BUCKET_DIST = BUCKET_THRESHOLDS[-1]
STRIP_ROW_BLOCK = 256


def _strip_body(tab_ref, o_ref, *, rows, row_step, const, max_dist):
    h = pl.program_id(0)
    r = pl.program_id(1)
    a = r * rows + lax.broadcasted_iota(jnp.int32, (rows, V7X_LANES), 0)
    l = lax.broadcasted_iota(jnp.int32, (rows, V7X_LANES), 1)
    dist = l + const - row_step * a
    v = jnp.full((rows, V7X_LANES), tab_ref[0, h], jnp.float32)
    for b, thr in enumerate(BUCKET_THRESHOLDS, start=1):
        v = jnp.where(dist >= thr, tab_ref[b, h], v)
    ok = dist >= 0
    if max_dist is not None:
        ok = jnp.logical_and(ok, dist < max_dist)
    o_ref[0] = jnp.where(ok, v * LOG2_E, NEG_INF)


def bias_strip(rel_table, min_rows, *, row_step, const, max_dist):
    n_heads = rel_table.shape[1]
    rows = STRIP_ROW_BLOCK
    n_rows = _round_up(min_rows, rows)
    return pl.pallas_call(
        functools.partial(_strip_body, rows=rows, row_step=row_step, const=const, max_dist=max_dist),
        grid=(n_heads, n_rows // rows),
        in_specs=[pl.BlockSpec(memory_space=pltpu.SMEM)],
        out_specs=pl.BlockSpec((1, rows, V7X_LANES), lambda h, r: (h, r, 0)),
        out_shape=jax.ShapeDtypeStruct((n_heads, n_rows, V7X_LANES), jnp.float32),
        compiler_params=_compiler_params(("arbitrary", "arbitrary")),
        name="bias_strip",
    )(rel_table)


SEL_STRIP_CONST = _round_up(LAST_BUCKET_DIST + V7X_SUBLANES - 1, V7X_SUBLANES)
SEL_STRIP_ROWS = SEL_STRIP_CONST + KEY_TILE
WIN_STRIP_CONST = _round_up(WINDOW + V7X_SUBLANES - 1, V7X_SUBLANES)
WIN_STRIP_ROWS = WIN_STRIP_CONST + Q_TILE
CMP_DIST_OFFSET = CMP_BLOCK - 1
CMP_STRIP_CONST = _round_up(LAST_BUCKET_DIST + CMP_STRIDE * (V7X_SUBLANES - 1) + CMP_DIST_OFFSET,
                            CMP_STRIDE * V7X_SUBLANES) - CMP_DIST_OFFSET
CMP_STRIP_ROW0 = (CMP_STRIP_CONST + CMP_DIST_OFFSET) // CMP_STRIDE
CMP_STRIP_HI = _round_up((V7X_LANES + CMP_STRIP_CONST) // CMP_STRIDE + 1, V7X_SUBLANES)
CMP_STRIP_ROWS = CMP_STRIP_HI + V7X_SUBLANES


def nsa_strips(rel_bias):
    sel_strip = bias_strip(rel_bias, SEL_STRIP_ROWS, row_step=1, const=SEL_STRIP_CONST, max_dist=None)
    win_strip = bias_strip(rel_bias, WIN_STRIP_ROWS, row_step=1, const=WIN_STRIP_CONST, max_dist=WINDOW)
    cmp_strip = bias_strip(rel_bias, CMP_STRIP_ROWS, row_step=CMP_STRIDE, const=CMP_STRIP_CONST, max_dist=None)
    return sel_strip, win_strip, cmp_strip


def _bias_columns(strip_ref, heads, offsets):
    cols = [jnp.concatenate([strip_ref[h, pl.ds(e, V7X_SUBLANES), :] for e in offsets], axis=0) for h in heads]
    return jnp.concatenate(cols, axis=1) if len(cols) > 1 else cols[0]


def _compress_body(x_ref, pos_ref, w1a_ref, w1b_ref, w2_ref, o_ref, acca_ref, accb_ref, *, n_chunks):
    l = pl.program_id(1)

    @pl.when(l == 0)
    def _():
        acca_ref[...] = jnp.zeros_like(acca_ref)
        accb_ref[...] = jnp.zeros_like(accb_ref)

    x = x_ref[...]
    pa = pos_ref[0, pl.ds(l, 1), :]
    pb = pos_ref[0, pl.ds(l + CMP_STRIDE, 1), :]
    acca_ref[...] += jnp.dot((x + pa).astype(jnp.bfloat16), w1a_ref[0], preferred_element_type=jnp.float32)
    accb_ref[...] += jnp.dot((x + pb).astype(jnp.bfloat16), w1b_ref[0], preferred_element_type=jnp.float32)

    @pl.when(l == CMP_STRIDE - 1)
    def _():
        pre = acca_ref[...] + pltpu.roll(accb_ref[...], n_chunks - 1, 0)
        act = jax.nn.gelu(pre)
        o_ref[0] = jnp.dot(act.astype(jnp.bfloat16), w2_ref[0],
                           preferred_element_type=jnp.float32).astype(o_ref.dtype)


def compress_kv(kv_cmp, pos, w1, w2):
    s = kv_cmp.shape[0]
    dh = HEAD_DIM
    n_chunks = s // CMP_STRIDE
    n_kvg = 2 * N_KV_GROUPS
    x = kv_cmp.reshape(n_chunks, CMP_STRIDE * n_kvg * dh)
    w1r = w1.reshape(2, CMP_BLOCK, dh, dh)
    return pl.pallas_call(
        functools.partial(_compress_body, n_chunks=n_chunks),
        grid=(n_kvg, CMP_STRIDE),
        in_specs=[
            pl.BlockSpec((n_chunks, dh), lambda c, l: (0, l * n_kvg + c)),
            pl.BlockSpec((1, CMP_BLOCK, dh), lambda c, l: (c // N_KV_GROUPS, 0, 0)),
            pl.BlockSpec((1, None, dh, dh), lambda c, l: (c // N_KV_GROUPS, l, 0, 0)),
            pl.BlockSpec((1, None, dh, dh), lambda c, l: (c // N_KV_GROUPS, l + CMP_STRIDE, 0, 0)),
            pl.BlockSpec((1, dh, dh), lambda c, l: (c // N_KV_GROUPS, 0, 0)),
        ],
        out_specs=pl.BlockSpec((1, n_chunks, dh), lambda c, l: (c, 0, 0)),
        out_shape=jax.ShapeDtypeStruct((n_kvg, n_chunks, dh), jnp.bfloat16),
        scratch_shapes=[pltpu.VMEM((n_chunks, dh), jnp.float32), pltpu.VMEM((n_chunks, dh), jnp.float32)],
        compiler_params=_compiler_params(("arbitrary", "arbitrary")),
        name="compress_kv",
    )(x, pos, w1r, w1r, w2)


def _online_softmax_tile(s, vt, m, acc_ref, skip=None):
    m_new = jnp.maximum(m, jnp.max(s, axis=0, keepdims=True))
    alpha = jnp.exp2(m - m_new)
    m_sub = m_new
    if skip is not None:
        m_new = jnp.where(skip, m, m_new)
        alpha = jnp.where(skip, 1.0, alpha)
        m_sub = jnp.where(skip, -NEG_INF, m_new)
    p = jnp.exp2(s - m_sub).astype(jnp.bfloat16)
    acc_ref[...] = alpha * acc_ref[...] + jnp.dot(vt, p, preferred_element_type=jnp.float32)
    return m_new


def _ones_rows(keys):
    row = lax.broadcasted_iota(jnp.int32, (AUG_ROWS, keys), 0)
    return jnp.where(row == 0, 1.0, 0.0).astype(jnp.bfloat16)


def _gate_row(gt_ref, first_row, heads):
    rows = gt_ref[pl.ds(pl.multiple_of(first_row, V7X_SUBLANES), V7X_SUBLANES), :]
    return jnp.concatenate([rows[h:h + 1, :] for h in range(heads)], axis=1)


def _nsa_body(q_ref, kc_ref, vct_ref, ks_ref, vst_ref, kw_ref, vwt_ref,
              cstrip_ref, sstrip_ref, wstrip_ref, gt_ref, poolt_ref, o_ref,
              acc_ref, out_ref, p_ref, selm_ref, sa_ref, sb_ref, *, heads, n_cmp, n_sel):
    g = pl.program_id(0)
    qb = pl.program_id(1)
    t0 = qb * Q_TILE
    dh = HEAD_DIM
    lanes = heads * Q_TILE
    heads_per_chunk = V7X_MXU_WIDTH // Q_TILE
    qt = jnp.concatenate([q_ref[h * dh:(h + 1) * dh, :] for h in range(heads)], axis=1)
    gate_base = g * heads

    def compressed_and_select(n_blocks):
        n_rows = n_blocks * (SEL_BLOCK // CMP_STRIDE)
        kc = kc_ref[0, :n_rows, :]
        cmp_offsets = [pl.multiple_of(jnp.clip(CMP_STRIP_ROW0 - qb * (Q_TILE // CMP_STRIDE) + m0, 0, CMP_STRIP_HI),
                                      V7X_SUBLANES) for m0 in range(0, n_rows, V7X_SUBLANES)]
        imp = jnp.zeros((n_rows, Q_TILE), jnp.float32)
        for c in range(heads // heads_per_chunk):
            hs = range(c * heads_per_chunk, (c + 1) * heads_per_chunk)
            lo, hi = c * V7X_MXU_WIDTH, (c + 1) * V7X_MXU_WIDTH
            s = jnp.dot(kc, qt[:, lo:hi], preferred_element_type=jnp.float32)
            s = s + _bias_columns(cstrip_ref, hs, cmp_offsets)
            m = jnp.max(s, axis=0, keepdims=True)
            e = jnp.exp2(s - m)
            l = jnp.maximum(jnp.sum(e, axis=0, keepdims=True), 1e-30)
            p = e * jnp.where(m > 0.5 * NEG_INF, 1.0 / l, 0.0)
            for k in range(heads_per_chunk):
                imp = imp + p[:, k * Q_TILE:(k + 1) * Q_TILE]
            p_ref[:n_rows, lo:hi] = p.astype(p_ref.dtype)
        gate_c = _gate_row(gt_ref, gate_base, heads)
        out_ref[...] = jnp.dot(vct_ref[0, :, :n_rows], p_ref[:n_rows, :], preferred_element_type=jnp.float32) * gate_c

        poolt = poolt_ref[:n_blocks, :n_rows]
        hi_t = imp.astype(jnp.bfloat16)
        r1 = imp - hi_t.astype(jnp.float32)
        mid_t = r1.astype(jnp.bfloat16)
        lo_t = (r1 - mid_t.astype(jnp.float32)).astype(jnp.bfloat16)
        imp_sel = (jnp.dot(poolt, hi_t, preferred_element_type=jnp.float32)
                   + jnp.dot(poolt, mid_t, preferred_element_type=jnp.float32)
                   + jnp.dot(poolt, lo_t, preferred_element_type=jnp.float32))

        j = lax.broadcasted_iota(jnp.int32, (n_blocks, Q_TILE), 0)
        t = t0 + lax.broadcasted_iota(jnp.int32, (n_blocks, Q_TILE), 1)
        cur = jnp.right_shift(t, SEL_BLOCK_LOG2)
        forced = jnp.logical_or(j == 0, jnp.logical_or(j == cur, j == cur - 1))
        valid = j * SEL_BLOCK <= t
        n_forced = 3
        score = jnp.where(jnp.logical_and(valid, jnp.logical_not(forced)), imp_sel, NEG_INF)
        picked = jnp.where(forced, 1.0, 0.0)
        jf = j.astype(jnp.float32)
        for _ in range(min(N_SELECT, n_sel) - n_forced):
            best = jnp.max(score, axis=0, keepdims=True)
            first = jnp.min(jnp.where(score == best, jf, float(n_blocks)), axis=0, keepdims=True)
            hit = jf == first
            picked = jnp.where(hit, 1.0, picked)
            score = jnp.where(hit, BELOW_NEG_INF, score)
        selm_ref[:n_blocks, :] = jnp.concatenate([picked - 1.0] * heads, axis=1)
        if n_blocks < n_sel:
            selm_ref[n_blocks:, :] = jnp.full((n_sel - n_blocks, lanes), -1.0, jnp.float32)

    blocks_step = n_sel // SELECT_VARIANTS
    variant = (t0 + Q_TILE - 1) // (SEL_BLOCK * blocks_step)
    for v in range(SELECT_VARIANTS):
        pl.when(variant == v)(functools.partial(compressed_and_select, (v + 1) * blocks_step))

    all_heads = range(heads)
    win_offsets = []
    for back in range(WIN_BLOCKS):
        for i0 in range(0, Q_TILE, V7X_SUBLANES):
            e = WIN_STRIP_CONST - back * Q_TILE + i0
            win_offsets.append(e if back == 0 else pl.multiple_of(jnp.where(back > qb, 0, e), V7X_SUBLANES))
    win_starts = [pl.multiple_of(jnp.maximum(qb - back, 0) * Q_TILE, Q_TILE) for back in range(WIN_BLOCKS)]
    k_w = jnp.concatenate([kw_ref[pl.ds(st, Q_TILE), :] for st in win_starts], axis=0)
    vt_w = jnp.concatenate([jnp.concatenate([vwt_ref[:, pl.ds(st, Q_TILE)] for st in win_starts], axis=1),
                            _ones_rows(WIN_BLOCKS * Q_TILE)], axis=0)
    s_w = jnp.dot(k_w, qt, preferred_element_type=jnp.float32) + _bias_columns(wstrip_ref, all_heads, win_offsets)
    p_w = jnp.exp2(s_w - jnp.max(s_w, axis=0, keepdims=True)).astype(jnp.bfloat16)
    acc_w = jnp.dot(vt_w, p_w, preferred_element_type=jnp.float32)
    gate_w = _gate_row(gt_ref, 2 * N_KV_GROUPS * heads + gate_base, heads)
    out_ref[...] += acc_w[:dh, :] * (gate_w / acc_w[dh:dh + 1, :])

    blocks_per_tile = KEY_TILE // SEL_BLOCK
    key_blk = jnp.right_shift(lax.broadcasted_iota(jnp.int32, (KEY_TILE, V7X_LANES), 0), SEL_BLOCK_LOG2)
    blk_col = lax.broadcasted_iota(jnp.int32, (KEY_TILE, V7X_LANES), 1)
    key_mask_cols = jnp.where(key_blk == blk_col, -NEG_INF, 0.0).astype(jnp.bfloat16)
    zero_rows = jnp.zeros((V7X_LANES - AUG_ROWS, lanes), jnp.bfloat16)
    sel_ones = _ones_rows(KEY_TILE)

    last_tile = (t0 + Q_TILE - 1) // KEY_TILE

    def scores(kt):
        ktc = jnp.minimum(kt, last_tile)
        start = pl.multiple_of(ktc * KEY_TILE, KEY_TILE)
        k_aug = jnp.concatenate([ks_ref[pl.ds(start, KEY_TILE), :], key_mask_cols], axis=1)
        sel_rows = selm_ref[pl.ds(pl.multiple_of(ktc * blocks_per_tile, blocks_per_tile), blocks_per_tile), :]
        sel_rows = jnp.concatenate([sel_rows, jnp.zeros((AUG_ROWS - blocks_per_tile, lanes), jnp.float32)], axis=0)
        qt_aug = jnp.concatenate([qt, sel_rows.astype(jnp.bfloat16), zero_rows], axis=0)
        row0 = SEL_STRIP_CONST - (t0 - ktc * KEY_TILE)
        offsets = [pl.multiple_of(jnp.maximum(row0 + i0, 0), V7X_SUBLANES) for i0 in range(0, KEY_TILE, V7X_SUBLANES)]
        s = jnp.dot(k_aug, qt_aug, preferred_element_type=jnp.float32)
        return s + _bias_columns(sstrip_ref, all_heads, offsets)

    def values(kt):
        start = pl.multiple_of(jnp.minimum(kt, last_tile) * KEY_TILE, KEY_TILE)
        return jnp.concatenate([vst_ref[:, pl.ds(start, KEY_TILE)], sel_ones], axis=0)

    def sel_step(i, m):
        kt = TILES_PER_TRIP * i
        for k in range(0, TILES_PER_TRIP, 2):
            sb_ref[...] = scores(kt + k + 1)
            m = _online_softmax_tile(sa_ref[...], values(kt + k), m, acc_ref,
                                     skip=None if k == 0 else kt + k > last_tile)
            sa_ref[...] = scores(kt + k + 2)
            m = _online_softmax_tile(sb_ref[...], values(kt + k + 1), m, acc_ref, skip=kt + k + 1 > last_tile)
        return m

    acc_ref[...] = jnp.zeros_like(acc_ref)
    m_init = jnp.full((1, lanes), NEG_INF, jnp.float32)
    sa_ref[...] = scores(0)
    lax.fori_loop(0, last_tile // TILES_PER_TRIP + 1, sel_step, m_init)
    gate_s = _gate_row(gt_ref, N_KV_GROUPS * heads + gate_base, heads)
    out_t = out_ref[...] + acc_ref[:dh, :] * (gate_s / acc_ref[dh:dh + 1, :])
    for h in range(heads):
        o_ref[:, h * dh:(h + 1) * dh] = out_t[:, h * Q_TILE:(h + 1) * Q_TILE].T.astype(o_ref.dtype)


def _pool_matrix_t(n_sel, n_cmp):
    r = SEL_BLOCK // CMP_STRIDE
    c = CMP_BLOCK // CMP_STRIDE
    j = jnp.arange(n_sel)[:, None]
    m = jnp.arange(n_cmp)[None, :]
    lo = r * j - (c - 1)
    return jnp.logical_and(m >= lo, m <= lo + r + c - 2).astype(jnp.bfloat16)


def nsa_attention(q_t, k_c, vt_c, k_sw, vt_sw, gates_t, strips):
    dq, s = q_t.shape
    dh = HEAD_DIM
    heads = dq // dh // N_KV_GROUPS
    assert heads == V7X_SUBLANES, "gate rows of one group must fill one sublane tile"
    assert KEY_TILE // SEL_BLOCK == V7X_SUBLANES, "one key tile's selection rows must fill one sublane tile"
    gw = heads * dh
    n_cmp = k_c.shape[1]
    n_sel = s // SEL_BLOCK
    sel_strip, win_strip, cmp_strip = strips
    once_per_group = pl.Buffered(1)

    def strip_spec(strip):
        return pl.BlockSpec((heads, strip.shape[1], V7X_LANES), lambda g, qb: (g, 0, 0), pipeline_mode=once_per_group)

    return pl.pallas_call(
        functools.partial(_nsa_body, heads=heads, n_cmp=n_cmp, n_sel=n_sel),
        grid=(N_KV_GROUPS, s // Q_TILE),
        in_specs=[
            pl.BlockSpec((gw, Q_TILE), lambda g, qb: (g, qb)),
            pl.BlockSpec((1, n_cmp, dh), lambda g, qb: (g, 0, 0)),
            pl.BlockSpec((1, dh, n_cmp), lambda g, qb: (g, 0, 0)),
            pl.BlockSpec((s, dh), lambda g, qb: (0, g), pipeline_mode=once_per_group),
            pl.BlockSpec((dh, s), lambda g, qb: (g, 0), pipeline_mode=once_per_group),
            pl.BlockSpec((s, dh), lambda g, qb: (0, N_KV_GROUPS + g), pipeline_mode=once_per_group),
            pl.BlockSpec((dh, s), lambda g, qb: (N_KV_GROUPS + g, 0), pipeline_mode=once_per_group),
            strip_spec(cmp_strip), strip_spec(sel_strip), strip_spec(win_strip),
            pl.BlockSpec((gates_t.shape[0], Q_TILE), lambda g, qb: (0, qb)),
            pl.BlockSpec((n_sel, n_cmp), lambda g, qb: (0, 0)),
        ],
        out_specs=pl.BlockSpec((Q_TILE, gw), lambda g, qb: (qb, g)),
        out_shape=jax.ShapeDtypeStruct((s, dq), jnp.bfloat16),
        scratch_shapes=[pltpu.VMEM((dh + AUG_ROWS, heads * Q_TILE), jnp.float32),
                        pltpu.VMEM((dh, heads * Q_TILE), jnp.float32),
                        pltpu.VMEM((n_cmp, heads * Q_TILE), jnp.bfloat16),
                        pltpu.VMEM((n_sel, heads * Q_TILE), jnp.float32),
                        pltpu.VMEM((KEY_TILE, heads * Q_TILE), jnp.float32),
                        pltpu.VMEM((KEY_TILE, heads * Q_TILE), jnp.float32)],
        compiler_params=_compiler_params(("arbitrary", "arbitrary")),
        name="nsa_attention",
    )(q_t, k_c, vt_c, k_sw, vt_sw, k_sw, vt_sw, cmp_strip, sel_strip, win_strip, gates_t,
      _pool_matrix_t(n_sel, n_cmp))


def _bf16(w):
    return w.astype(jnp.bfloat16)


def nsa_mixer(h, w_in, cmp_pos, cmp_w1, cmp_w2, w_out_all, layer, strips):
    dh, g = HEAD_DIM, N_KV_GROUPS
    q_end = w_out_all.shape[1]
    half = g * dh
    kv_end = q_end + N_BRANCHES * 2 * half
    n_gates = w_in.shape[1] - kv_end
    scale2 = dh ** -0.5 * LOG2_E

    def cols(branch, is_v):
        lo = q_end + branch * 2 * half + (half if is_v else 0)
        return w_in[:, lo:lo + half]

    q_t = matmul_nt(_bf16(w_in[:, :q_end].T), h, jnp.bfloat16, out_scale=scale2, name="nsa_q_proj")
    kv_cmp = matmul(h, _bf16(w_in[:, q_end:q_end + 2 * half]), jnp.float32, name="nsa_kv_cmp_proj")
    k_sw = matmul(h, _bf16(jnp.concatenate([cols(1, False), cols(2, False)], axis=1)), jnp.bfloat16,
                  name="nsa_k_proj")
    vt_sw = matmul_nt(_bf16(jnp.concatenate([cols(1, True), cols(2, True)], axis=1).T), h, jnp.bfloat16,
                      name="nsa_vt_proj")
    w_gate_t = jnp.pad(_bf16(w_in[:, kv_end:].T), ((0, V7X_LANES - n_gates), (0, 0)))
    gates_t = matmul_nt(w_gate_t, h, jnp.float32, sigmoid=True, name="nsa_gate_proj")

    kv_c = compress_kv(kv_cmp, cmp_pos, _bf16(cmp_w1), _bf16(cmp_w2))
    k_c = kv_c[:g]
    vt_c = jnp.swapaxes(kv_c[g:], 1, 2)
    o = nsa_attention(q_t, k_c, vt_c, k_sw, vt_sw, gates_t, strips)
    return matmul(o, w_out_all, jnp.float32, (layer,), name="nsa_out_proj")


def pool_mixer(h, w_in_all, w_group_all, scale, w_out_all, layer):
    y = pool_front(h, w_in_all, w_group_all, scale, layer)
    return matmul(y, w_out_all, jnp.float32, (layer,), name="pool_out_proj")


def kernel(x, norm_gains, ffn_w_gate, ffn_w_up, ffn_w_down, pool_w_in, pool_w_group, pool_scale, pool_w_out,
           nsa_w_in, nsa_cmp_pos, nsa_cmp_w1, nsa_cmp_w2, nsa_w_out, rel_bias):
    b, s, d = x.shape
    depth = norm_gains.shape[0]
    outs = []
    strips = nsa_strips(rel_bias) if depth > 1 else None
    wg, wu, wd = _bf16(ffn_w_gate), _bf16(ffn_w_up), _bf16(ffn_w_down)
    p_in, p_grp, p_out, n_out = _bf16(pool_w_in), _bf16(pool_w_group), _bf16(pool_w_out), _bf16(nsa_w_out)
    for bi in range(b):
        xs = x[bi]
        h = rms_norm_cast(xs, norm_gains[0, 0])
        for i in range(depth):
            gains = norm_gains[i]
            f1 = swiglu_ffn(h, wg, wu, wd, (i, 0))
            xs, h = residual_norm(xs, f1, gains[1], gains[2], 0.5)
            li = i // 2
            if i % 2 == 0:
                m = pool_mixer(h, p_in, p_grp, pool_scale[li], p_out, li)
            else:
                m = nsa_mixer(h, nsa_w_in[li], nsa_cmp_pos[li], nsa_cmp_w1[li], nsa_cmp_w2[li], n_out, li, strips)
            xs, h = residual_norm(xs, m, gains[3], gains[4], 1.0)
            f2 = swiglu_ffn(h, wg, wu, wd, (i, 1))
            g_next = norm_gains[i + 1, 0] if i + 1 < depth else None
            xs, h = residual_norm(xs, f2, gains[5], g_next, 0.5)
        outs.append(xs)
    return jnp.stack(outs, axis=0)
```

```python
import functools
import math

import jax
import jax.numpy as jnp
from jax import lax
from jax.experimental import pallas as pl
from jax.experimental.pallas import tpu as pltpu

NORM_EPS = 1e-6
POOL_WINDOWS = (2, 4, 8, 16)
HEAD_DIM = 128
N_KV_GROUPS = 4
N_BRANCHES = 3
CMP_BLOCK = 32
CMP_STRIDE = 16
SEL_BLOCK = 64
SEL_BLOCK_LOG2 = 6
N_SELECT = 16
WINDOW = 512
N_BUCKETS = 32
MAX_EXACT = N_BUCKETS // 2
NEG_INF = -1e30
BELOW_NEG_INF = -3e38
LOG2_E = math.log2(math.e)

V7X_LANES = 128
V7X_SUBLANES = 8
V7X_MXU_WIDTH = 256
V7X_VMEM_LIMIT_BYTES = 56 * 1024 * 1024

Q_TILE = 128
KEY_TILE = 512
SELECT_VARIANTS = 4
WIN_BLOCKS = WINDOW // Q_TILE + 1
AUG_ROWS = 16


def _compiler_params(semantics):
    return pltpu.CompilerParams(dimension_semantics=semantics, vmem_limit_bytes=V7X_VMEM_LIMIT_BYTES)


def _round_up(x, m):
    return -(-x // m) * m


def _rms(xf, gain):
    ms = jnp.mean(xf * xf, axis=-1, keepdims=True)
    return xf * lax.rsqrt(ms + NORM_EPS) * gain


_NT = (((1,), (1,)), ((), ()))


def _norm_body(x_ref, g_ref, h_ref):
    h_ref[...] = _rms(x_ref[...], g_ref[...]).astype(h_ref.dtype)


def rms_norm_cast(x, gain, *, rows=256):
    s, d = x.shape
    return pl.pallas_call(
        _norm_body,
        grid=(s // rows,),
        in_specs=[pl.BlockSpec((rows, d), lambda i: (i, 0)), pl.BlockSpec((1, d), lambda i: (0, 0))],
        out_specs=pl.BlockSpec((rows, d), lambda i: (i, 0)),
        out_shape=jax.ShapeDtypeStruct((s, d), jnp.bfloat16),
        compiler_params=_compiler_params(("arbitrary",)),
        name="rms_norm_cast",
    )(x, gain.reshape(1, d))


def _resnorm_body(x_ref, f_ref, gp_ref, gn_ref, xo_ref, h_ref, *, alpha):
    y = x_ref[...] + alpha * _rms(f_ref[...], gp_ref[...])
    xo_ref[...] = y
    h_ref[...] = _rms(y, gn_ref[...]).astype(h_ref.dtype)


def _resnorm_last_body(x_ref, f_ref, gp_ref, xo_ref, *, alpha):
    xo_ref[...] = x_ref[...] + alpha * _rms(f_ref[...], gp_ref[...])


def residual_norm(x, f, g_post, g_next, alpha, *, rows=256):
    s, d = x.shape
    row_spec = pl.BlockSpec((rows, d), lambda i: (i, 0))
    gain_spec = pl.BlockSpec((1, d), lambda i: (0, 0))
    if g_next is None:
        return pl.pallas_call(
            functools.partial(_resnorm_last_body, alpha=alpha),
            grid=(s // rows,),
            in_specs=[row_spec, row_spec, gain_spec],
            out_specs=row_spec,
            out_shape=jax.ShapeDtypeStruct((s, d), jnp.float32),
            compiler_params=_compiler_params(("arbitrary",)),
            name="residual_norm_last",
        )(x, f, g_post.reshape(1, d)), None
    return pl.pallas_call(
        functools.partial(_resnorm_body, alpha=alpha),
        grid=(s // rows,),
        in_specs=[row_spec, row_spec, gain_spec, gain_spec],
        out_specs=[row_spec, row_spec],
        out_shape=[jax.ShapeDtypeStruct((s, d), jnp.float32), jax.ShapeDtypeStruct((s, d), jnp.bfloat16)],
        compiler_params=_compiler_params(("arbitrary",)),
        name="residual_norm",
    )(x, f, g_post.reshape(1, d), g_next.reshape(1, d))


def _ffn_body(h_ref, wg_ref, wu_ref, wd_ref, o_ref, a_ref, *, n_up, tf):
    j = pl.program_id(1)

    @pl.when(j < n_up)
    def _():
        h = h_ref[...]
        g = jnp.dot(h, wg_ref[...], preferred_element_type=jnp.float32)
        u = jnp.dot(h, wu_ref[...], preferred_element_type=jnp.float32)
        a = (g * jax.nn.sigmoid(g)) * u
        a_ref[:, pl.ds(pl.multiple_of(j * tf, tf), tf)] = a.astype(a_ref.dtype)

    @pl.when(j >= n_up)
    def _():
        o_ref[...] = jnp.dot(a_ref[...], wd_ref[...], preferred_element_type=jnp.float32)


def swiglu_ffn(h, w_gate, w_up, w_down, index=(), *, tm=1024, tf=256, tn=256):
    s, d = h.shape
    f = w_gate.shape[-1]
    tm = min(tm, s)
    n_up, n_down = f // tf, d // tn
    lead = (None,) * len(index)
    return pl.pallas_call(
        functools.partial(_ffn_body, n_up=n_up, tf=tf),
        grid=(s // tm, n_up + n_down),
        in_specs=[
            pl.BlockSpec((tm, d), lambda i, j: (i, 0)),
            pl.BlockSpec(lead + (d, tf), lambda i, j: (*index, 0, jnp.minimum(j, n_up - 1))),
            pl.BlockSpec(lead + (d, tf), lambda i, j: (*index, 0, jnp.minimum(j, n_up - 1))),
            pl.BlockSpec(lead + (f, tn), lambda i, j: (*index, 0, jnp.maximum(j - n_up, 0))),
        ],
        out_specs=pl.BlockSpec((tm, tn), lambda i, j: (i, jnp.maximum(j - n_up, 0))),
        out_shape=jax.ShapeDtypeStruct((s, d), jnp.float32),
        scratch_shapes=[pltpu.VMEM((tm, f), jnp.bfloat16)],
        compiler_params=_compiler_params(("arbitrary", "arbitrary")),
        name="swiglu_ffn",
    )(h, w_gate, w_up, w_down)


def _matmul_body(a_ref, b_ref, o_ref):
    o_ref[...] = jnp.dot(a_ref[...], b_ref[...], preferred_element_type=jnp.float32).astype(o_ref.dtype)


def matmul(a, b, out_dtype, index=(), *, tm=1024, tn=512, name="matmul"):
    m, k = a.shape
    n = b.shape[-1]
    tm, tn = min(tm, m), min(tn, n)
    lead = (None,) * len(index)
    return pl.pallas_call(
        _matmul_body,
        grid=(m // tm, n // tn),
        in_specs=[pl.BlockSpec((tm, k), lambda i, j: (i, 0)),
                  pl.BlockSpec(lead + (k, tn), lambda i, j: (*index, 0, j))],
        out_specs=pl.BlockSpec((tm, tn), lambda i, j: (i, j)),
        out_shape=jax.ShapeDtypeStruct((m, n), out_dtype),
        compiler_params=_compiler_params(("arbitrary", "arbitrary")),
        name=name,
    )(a, b)


def _matmul_nt_body(bt_ref, a_ref, o_ref, *, sigmoid, out_scale):
    r = lax.dot_general(bt_ref[...], a_ref[...], _NT, preferred_element_type=jnp.float32)
    if sigmoid:
        r = jax.nn.sigmoid(r)
    if out_scale is not None:
        r = r * out_scale
    o_ref[...] = r.astype(o_ref.dtype)


def matmul_nt(bt, a, out_dtype, *, tm=1024, tn=512, sigmoid=False, out_scale=None, name="matmul_nt"):
    n, k = bt.shape
    m = a.shape[0]
    tm, tn = min(tm, m), min(tn, n)
    return pl.pallas_call(
        functools.partial(_matmul_nt_body, sigmoid=sigmoid, out_scale=out_scale),
        grid=(m // tm, n // tn),
        in_specs=[pl.BlockSpec((tn, k), lambda i, j: (j, 0)), pl.BlockSpec((tm, k), lambda i, j: (i, 0))],
        out_specs=pl.BlockSpec((tn, tm), lambda i, j: (j, i)),
        out_shape=jax.ShapeDtypeStruct((n, m), out_dtype),
        compiler_params=_compiler_params(("arbitrary", "arbitrary")),
        name=name,
    )(bt, a)


POOL_HALO = 16


def _pool_body(h_ref, win_ref, wgrp_ref, scale_ref, y_ref, halo_ref, *, tm):
    grp = pl.program_id(0)
    i = pl.program_id(1)
    u = jnp.dot(h_ref[...], win_ref[...], preferred_element_type=jnp.float32)

    @pl.when(i == 0)
    def _():
        halo_ref[...] = jnp.zeros_like(halo_ref)

    ext = jnp.concatenate([halo_ref[...], u], axis=0)
    halo_ref[...] = u[tm - POOL_HALO:, :]
    s2 = ext + pltpu.roll(ext, 1, 0)
    s4 = s2 + pltpu.roll(s2, 2, 0)
    s8 = s4 + pltpu.roll(s4, 4, 0)
    s16 = s8 + pltpu.roll(s8, 8, 0)
    sw = jnp.where(grp == 0, s2, jnp.where(grp == 1, s4, jnp.where(grp == 2, s8, s16)))[POOL_HALO:, :]
    w = jnp.where(grp == 0, POOL_WINDOWS[0], jnp.where(grp == 1, POOL_WINDOWS[1],
                  jnp.where(grp == 2, POOL_WINDOWS[2], POOL_WINDOWS[3])))
    t = i * tm + lax.broadcasted_iota(jnp.int32, (tm, 1), 0)
    count = jnp.minimum(t + 1, w).astype(jnp.float32)
    d = sw / count - u
    y = jnp.dot(d.astype(jnp.bfloat16), wgrp_ref[0], preferred_element_type=jnp.float32)
    y_ref[...] = (y * scale_ref[...]).astype(y_ref.dtype)


def pool_front(h, w_in, w_group, scale, layer, *, tm=512):
    s, d = h.shape
    _, n_groups, gw, _ = w_group.shape
    tm = min(tm, s)
    return pl.pallas_call(
        functools.partial(_pool_body, tm=tm),
        grid=(n_groups, s // tm),
        in_specs=[
            pl.BlockSpec((tm, d), lambda g, i: (i, 0)),
            pl.BlockSpec((None, d, gw), lambda g, i: (layer, 0, g)),
            pl.BlockSpec((None, 1, gw, gw), lambda g, i: (layer, g, 0, 0)),
            pl.BlockSpec((1, gw), lambda g, i: (0, g)),
        ],
        out_specs=pl.BlockSpec((tm, gw), lambda g, i: (i, g)),
        out_shape=jax.ShapeDtypeStruct((s, d), jnp.bfloat16),
        scratch_shapes=[pltpu.VMEM((POOL_HALO, gw), jnp.float32)],
        compiler_params=_compiler_params(("arbitrary", "arbitrary")),
        name="pool_front",
    )(h, w_in, w_group, scale.reshape(1, d))


def _bucket_thresholds():
    thr = list(range(1, MAX_EXACT + 1))
    for k in range(1, N_BUCKETS - MAX_EXACT):
        thr.append(math.isqrt(256 * 2 ** k - 1) + 1)
    return tuple(thr)


BUCKET_THRESHOLDS = _bucket_thresholds()
LAST_BUCKET_DIST = BUCKET_THRESHOLDS[-1]
STRIP_ROW_BLOCK = 256


def _strip_body(tab_ref, o_ref, *, rows, row_step, const, max_dist):
    h = pl.program_id(0)
    r = pl.program_id(1)
    a = r * rows + lax.broadcasted_iota(jnp.int32, (rows, V7X_LANES), 0)
    l = lax.broadcasted_iota(jnp.int32, (rows, V7X_LANES), 1)
    dist = l + const - row_step * a
    v = jnp.full((rows, V7X_LANES), tab_ref[0, h], jnp.float32)
    for b, thr in enumerate(BUCKET_THRESHOLDS, start=1):
        v = jnp.where(dist >= thr, tab_ref[b, h], v)
    ok = dist >= 0
    if max_dist is not None:
        ok = jnp.logical_and(ok, dist < max_dist)
    o_ref[0] = jnp.where(ok, v * LOG2_E, NEG_INF)


def bias_strip(rel_table, min_rows, *, row_step, const, max_dist):
    n_heads = rel_table.shape[1]
    rows = STRIP_ROW_BLOCK
    n_rows = _round_up(min_rows, rows)
    return pl.pallas_call(
        functools.partial(_strip_body, rows=rows, row_step=row_step, const=const, max_dist=max_dist),
        grid=(n_heads, n_rows // rows),
        in_specs=[pl.BlockSpec(memory_space=pltpu.SMEM)],
        out_specs=pl.BlockSpec((1, rows, V7X_LANES), lambda h, r: (h, r, 0)),
        out_shape=jax.ShapeDtypeStruct((n_heads, n_rows, V7X_LANES), jnp.float32),
        compiler_params=_compiler_params(("arbitrary", "arbitrary")),
        name="bias_strip",
    )(rel_table)


SEL_STRIP_CONST = _round_up(LAST_BUCKET_DIST + V7X_SUBLANES - 1, V7X_SUBLANES)
SEL_STRIP_ROWS = SEL_STRIP_CONST + KEY_TILE
WIN_STRIP_CONST = _round_up(WINDOW + V7X_SUBLANES - 1, V7X_SUBLANES)
WIN_STRIP_ROWS = WIN_STRIP_CONST + Q_TILE
CMP_DIST_OFFSET = CMP_BLOCK - 1
CMP_STRIP_CONST = _round_up(LAST_BUCKET_DIST + CMP_STRIDE * (V7X_SUBLANES - 1) + CMP_DIST_OFFSET,
                            CMP_STRIDE * V7X_SUBLANES) - CMP_DIST_OFFSET
CMP_STRIP_ROW0 = (CMP_STRIP_CONST + CMP_DIST_OFFSET) // CMP_STRIDE
CMP_STRIP_HI = _round_up((V7X_LANES + CMP_STRIP_CONST) // CMP_STRIDE + 1, V7X_SUBLANES)
CMP_STRIP_ROWS = CMP_STRIP_HI + V7X_SUBLANES


def nsa_strips(rel_bias):
    sel_strip = bias_strip(rel_bias, SEL_STRIP_ROWS, row_step=1, const=SEL_STRIP_CONST, max_dist=None)
    win_strip = bias_strip(rel_bias, WIN_STRIP_ROWS, row_step=1, const=WIN_STRIP_CONST, max_dist=WINDOW)
    cmp_strip = bias_strip(rel_bias, CMP_STRIP_ROWS, row_step=CMP_STRIDE, const=CMP_STRIP_CONST, max_dist=None)
    return sel_strip, win_strip, cmp_strip


def _bias_columns(strip_ref, heads, offsets):
    cols = [jnp.concatenate([strip_ref[h, pl.ds(e, V7X_SUBLANES), :] for e in offsets], axis=0) for h in heads]
    return jnp.concatenate(cols, axis=1) if len(cols) > 1 else cols[0]


def _compress_body(x_ref, pos_ref, w1a_ref, w1b_ref, w2_ref, o_ref, acca_ref, accb_ref, *, n_chunks):
    l = pl.program_id(1)

    @pl.when(l == 0)
    def _():
        acca_ref[...] = jnp.zeros_like(acca_ref)
        accb_ref[...] = jnp.zeros_like(accb_ref)

    x = x_ref[...]
    pa = pos_ref[0, pl.ds(l, 1), :]
    pb = pos_ref[0, pl.ds(l + CMP_STRIDE, 1), :]
    acca_ref[...] += jnp.dot((x + pa).astype(jnp.bfloat16), w1a_ref[0], preferred_element_type=jnp.float32)
    accb_ref[...] += jnp.dot((x + pb).astype(jnp.bfloat16), w1b_ref[0], preferred_element_type=jnp.float32)

    @pl.when(l == CMP_STRIDE - 1)
    def _():
        pre = acca_ref[...] + pltpu.roll(accb_ref[...], n_chunks - 1, 0)
        act = jax.nn.gelu(pre)
        o_ref[0] = jnp.dot(act.astype(jnp.bfloat16), w2_ref[0],
                           preferred_element_type=jnp.float32).astype(o_ref.dtype)


def compress_kv(kv_cmp, pos, w1, w2):
    s = kv_cmp.shape[0]
    dh = HEAD_DIM
    n_chunks = s // CMP_STRIDE
    n_kvg = 2 * N_KV_GROUPS
    x = kv_cmp.reshape(n_chunks, CMP_STRIDE * n_kvg * dh)
    w1r = w1.reshape(2, CMP_BLOCK, dh, dh)
    return pl.pallas_call(
        functools.partial(_compress_body, n_chunks=n_chunks),
        grid=(n_kvg, CMP_STRIDE),
        in_specs=[
            pl.BlockSpec((n_chunks, dh), lambda c, l: (0, l * n_kvg + c)),
            pl.BlockSpec((1, CMP_BLOCK, dh), lambda c, l: (c // N_KV_GROUPS, 0, 0)),
            pl.BlockSpec((1, None, dh, dh), lambda c, l: (c // N_KV_GROUPS, l, 0, 0)),
            pl.BlockSpec((1, None, dh, dh), lambda c, l: (c // N_KV_GROUPS, l + CMP_STRIDE, 0, 0)),
            pl.BlockSpec((1, dh, dh), lambda c, l: (c // N_KV_GROUPS, 0, 0)),
        ],
        out_specs=pl.BlockSpec((1, n_chunks, dh), lambda c, l: (c, 0, 0)),
        out_shape=jax.ShapeDtypeStruct((n_kvg, n_chunks, dh), jnp.bfloat16),
        scratch_shapes=[pltpu.VMEM((n_chunks, dh), jnp.float32), pltpu.VMEM((n_chunks, dh), jnp.float32)],
        compiler_params=_compiler_params(("arbitrary", "arbitrary")),
        name="compress_kv",
    )(x, pos, w1r, w1r, w2)


def _online_softmax_tile(s, vt, m, acc_ref):
    m_new = jnp.maximum(m, jnp.max(s, axis=0, keepdims=True))
    alpha = jnp.exp2(m - m_new)
    p = jnp.exp2(s - m_new).astype(jnp.bfloat16)
    acc_ref[...] = alpha * acc_ref[...] + jnp.dot(vt, p, preferred_element_type=jnp.float32)
    return m_new


def _ones_rows(keys):
    row = lax.broadcasted_iota(jnp.int32, (AUG_ROWS, keys), 0)
    return jnp.where(row == 0, 1.0, 0.0).astype(jnp.bfloat16)


def _gate_row(gt_ref, first_row, heads):
    rows = gt_ref[pl.ds(pl.multiple_of(first_row, V7X_SUBLANES), V7X_SUBLANES), :]
    return jnp.concatenate([rows[h:h + 1, :] for h in range(heads)], axis=1)


def _nsa_body(q_ref, kc_ref, vct_ref, ks_ref, vst_ref, kw_ref, vwt_ref,
              cstrip_ref, sstrip_ref, wstrip_ref, gt_ref, poolt_ref, o_ref,
              acc_ref, out_ref, p_ref, selm_ref, sa_ref, sb_ref, *, heads, n_cmp, n_sel):
    g = pl.program_id(0)
    qb = pl.program_id(1)
    t0 = qb * Q_TILE
    dh = HEAD_DIM
    lanes = heads * Q_TILE
    heads_per_chunk = V7X_MXU_WIDTH // Q_TILE
    qt = jnp.concatenate([q_ref[h * dh:(h + 1) * dh, :] for h in range(heads)], axis=1)
    gate_base = g * heads

    def compressed_and_select(n_blocks):
        n_rows = n_blocks * (SEL_BLOCK // CMP_STRIDE)
        kc = kc_ref[0, :n_rows, :]
        cmp_offsets = [pl.multiple_of(jnp.clip(CMP_STRIP_ROW0 - qb * (Q_TILE // CMP_STRIDE) + m0, 0, CMP_STRIP_HI),
                                      V7X_SUBLANES) for m0 in range(0, n_rows, V7X_SUBLANES)]
        imp = jnp.zeros((n_rows, Q_TILE), jnp.float32)
        for c in range(heads // heads_per_chunk):
            hs = range(c * heads_per_chunk, (c + 1) * heads_per_chunk)
            lo, hi = c * V7X_MXU_WIDTH, (c + 1) * V7X_MXU_WIDTH
            s = jnp.dot(kc, qt[:, lo:hi], preferred_element_type=jnp.float32)
            s = s + _bias_columns(cstrip_ref, hs, cmp_offsets)
            m = jnp.max(s, axis=0, keepdims=True)
            e = jnp.exp2(s - m)
            l = jnp.maximum(jnp.sum(e, axis=0, keepdims=True), 1e-30)
            p = e * jnp.where(m > 0.5 * NEG_INF, 1.0 / l, 0.0)
            for k in range(heads_per_chunk):
                imp = imp + p[:, k * Q_TILE:(k + 1) * Q_TILE]
            p_ref[:n_rows, lo:hi] = p.astype(p_ref.dtype)
        gate_c = _gate_row(gt_ref, gate_base, heads)
        out_ref[...] = jnp.dot(vct_ref[0, :, :n_rows], p_ref[:n_rows, :], preferred_element_type=jnp.float32) * gate_c

        poolt = poolt_ref[:n_blocks, :n_rows]
        hi_t = imp.astype(jnp.bfloat16)
        r1 = imp - hi_t.astype(jnp.float32)
        mid_t = r1.astype(jnp.bfloat16)
        lo_t = (r1 - mid_t.astype(jnp.float32)).astype(jnp.bfloat16)
        imp_sel = (jnp.dot(poolt, hi_t, preferred_element_type=jnp.float32)
                   + jnp.dot(poolt, mid_t, preferred_element_type=jnp.float32)
                   + jnp.dot(poolt, lo_t, preferred_element_type=jnp.float32))

        j = lax.broadcasted_iota(jnp.int32, (n_blocks, Q_TILE), 0)
        t = t0 + lax.broadcasted_iota(jnp.int32, (n_blocks, Q_TILE), 1)
        cur = jnp.right_shift(t, SEL_BLOCK_LOG2)
        forced = jnp.logical_or(j == 0, jnp.logical_or(j == cur, j == cur - 1))
        valid = j * SEL_BLOCK <= t
        n_forced = 3
        score = jnp.where(jnp.logical_and(valid, jnp.logical_not(forced)), imp_sel, NEG_INF)
        picked = jnp.where(forced, 1.0, 0.0)
        jf = j.astype(jnp.float32)
        for _ in range(min(N_SELECT, n_sel) - n_forced):
            best = jnp.max(score, axis=0, keepdims=True)
            first = jnp.min(jnp.where(score == best, jf, float(n_blocks)), axis=0, keepdims=True)
            hit = jf == first
            picked = jnp.where(hit, 1.0, picked)
            score = jnp.where(hit, BELOW_NEG_INF, score)
        selm_ref[:n_blocks, :] = jnp.concatenate([picked - 1.0] * heads, axis=1)
        if n_blocks < n_sel:
            selm_ref[n_blocks:, :] = jnp.full((n_sel - n_blocks, lanes), -1.0, jnp.float32)

    blocks_step = n_sel // SELECT_VARIANTS
    variant = (t0 + Q_TILE - 1) // (SEL_BLOCK * blocks_step)
    for v in range(SELECT_VARIANTS):
        pl.when(variant == v)(functools.partial(compressed_and_select, (v + 1) * blocks_step))

    all_heads = range(heads)
    win_offsets = []
    for back in range(WIN_BLOCKS):
        for i0 in range(0, Q_TILE, V7X_SUBLANES):
            e = WIN_STRIP_CONST - back * Q_TILE + i0
            win_offsets.append(e if back == 0 else pl.multiple_of(jnp.where(back > qb, 0, e), V7X_SUBLANES))
    win_starts = [pl.multiple_of(jnp.maximum(qb - back, 0) * Q_TILE, Q_TILE) for back in range(WIN_BLOCKS)]
    k_w = jnp.concatenate([kw_ref[pl.ds(st, Q_TILE), :] for st in win_starts], axis=0)
    vt_w = jnp.concatenate([jnp.concatenate([vwt_ref[:, pl.ds(st, Q_TILE)] for st in win_starts], axis=1),
                            _ones_rows(WIN_BLOCKS * Q_TILE)], axis=0)
    s_w = jnp.dot(k_w, qt, preferred_element_type=jnp.float32) + _bias_columns(wstrip_ref, all_heads, win_offsets)
    p_w = jnp.exp2(s_w - jnp.max(s_w, axis=0, keepdims=True)).astype(jnp.bfloat16)
    acc_w = jnp.dot(vt_w, p_w, preferred_element_type=jnp.float32)
    gate_w = _gate_row(gt_ref, 2 * N_KV_GROUPS * heads + gate_base, heads)
    out_ref[...] += acc_w[:dh, :] * (gate_w / acc_w[dh:dh + 1, :])

    blocks_per_tile = KEY_TILE // SEL_BLOCK
    key_blk = jnp.right_shift(lax.broadcasted_iota(jnp.int32, (KEY_TILE, V7X_LANES), 0), SEL_BLOCK_LOG2)
    blk_col = lax.broadcasted_iota(jnp.int32, (KEY_TILE, V7X_LANES), 1)
    key_mask_cols = jnp.where(key_blk == blk_col, -NEG_INF, 0.0).astype(jnp.bfloat16)
    zero_rows = jnp.zeros((V7X_LANES - AUG_ROWS, lanes), jnp.bfloat16)
    sel_ones = _ones_rows(KEY_TILE)

    last_tile = (t0 + Q_TILE - 1) // KEY_TILE

    def scores(kt):
        start = pl.multiple_of(kt * KEY_TILE, KEY_TILE)
        k_aug = jnp.concatenate([ks_ref[pl.ds(start, KEY_TILE), :], key_mask_cols], axis=1)
        sel_rows = selm_ref[pl.ds(pl.multiple_of(kt * blocks_per_tile, blocks_per_tile), blocks_per_tile), :]
        sel_rows = jnp.concatenate([sel_rows, jnp.zeros((AUG_ROWS - blocks_per_tile, lanes), jnp.float32)], axis=0)
        qt_aug = jnp.concatenate([qt, sel_rows.astype(jnp.bfloat16), zero_rows], axis=0)
        row0 = SEL_STRIP_CONST - (t0 - kt * KEY_TILE)
        offsets = [pl.multiple_of(jnp.maximum(row0 + i0, 0), V7X_SUBLANES) for i0 in range(0, KEY_TILE, V7X_SUBLANES)]
        s = jnp.dot(k_aug, qt_aug, preferred_element_type=jnp.float32)
        return s + _bias_columns(sstrip_ref, all_heads, offsets)

    def values(kt):
        start = pl.multiple_of(kt * KEY_TILE, KEY_TILE)
        return jnp.concatenate([vst_ref[:, pl.ds(start, KEY_TILE)], sel_ones], axis=0)

    def sel_step(i, m):
        kt = 2 * i
        sb_ref[...] = scores(kt + 1)
        m = _online_softmax_tile(sa_ref[...], values(kt), m, acc_ref)
        sa_ref[...] = scores(kt + 2)
        return _online_softmax_tile(sb_ref[...], values(kt + 1), m, acc_ref)

    acc_ref[...] = jnp.zeros_like(acc_ref)
    sa_ref[...] = scores(0)
    m_run = lax.fori_loop(0, last_tile // 2, sel_step, jnp.full((1, lanes), NEG_INF, jnp.float32))
    tail = 2 * (last_tile // 2)

    @pl.when(tail < last_tile)
    def _():
        sb_ref[...] = scores(tail + 1)
        m = _online_softmax_tile(sa_ref[...], values(tail), m_run, acc_ref)
        _online_softmax_tile(sb_ref[...], values(tail + 1), m, acc_ref)

    @pl.when(tail == last_tile)
    def _():
        _online_softmax_tile(sa_ref[...], values(tail), m_run, acc_ref)

    gate_s = _gate_row(gt_ref, N_KV_GROUPS * heads + gate_base, heads)
    out_t = out_ref[...] + acc_ref[:dh, :] * (gate_s / acc_ref[dh:dh + 1, :])
    for h in range(heads):
        o_ref[:, h * dh:(h + 1) * dh] = out_t[:, h * Q_TILE:(h + 1) * Q_TILE].T.astype(o_ref.dtype)


def _pool_matrix_t(n_sel, n_cmp):
    r = SEL_BLOCK // CMP_STRIDE
    c = CMP_BLOCK // CMP_STRIDE
    j = jnp.arange(n_sel)[:, None]
    m = jnp.arange(n_cmp)[None, :]
    lo = r * j - (c - 1)
    return jnp.logical_and(m >= lo, m <= lo + r + c - 2).astype(jnp.bfloat16)


def nsa_attention(q_t, k_c, vt_c, k_sw, vt_sw, gates_t, strips):
    dq, s = q_t.shape
    dh = HEAD_DIM
    heads = dq // dh // N_KV_GROUPS
    assert heads == V7X_SUBLANES, "gate rows of one group must fill one sublane tile"
    assert KEY_TILE // SEL_BLOCK == V7X_SUBLANES, "one key tile's selection rows must fill one sublane tile"
    gw = heads * dh
    n_cmp = k_c.shape[1]
    n_sel = s // SEL_BLOCK
    sel_strip, win_strip, cmp_strip = strips
    once_per_group = pl.Buffered(1)

    def strip_spec(strip):
        return pl.BlockSpec((heads, strip.shape[1], V7X_LANES), lambda g, qb: (g, 0, 0), pipeline_mode=once_per_group)

    return pl.pallas_call(
        functools.partial(_nsa_body, heads=heads, n_cmp=n_cmp, n_sel=n_sel),
        grid=(N_KV_GROUPS, s // Q_TILE),
        in_specs=[
            pl.BlockSpec((gw, Q_TILE), lambda g, qb: (g, qb)),
            pl.BlockSpec((1, n_cmp, dh), lambda g, qb: (g, 0, 0)),
            pl.BlockSpec((1, dh, n_cmp), lambda g, qb: (g, 0, 0)),
            pl.BlockSpec((s, dh), lambda g, qb: (0, g), pipeline_mode=once_per_group),
            pl.BlockSpec((dh, s), lambda g, qb: (g, 0), pipeline_mode=once_per_group),
            pl.BlockSpec((s, dh), lambda g, qb: (0, N_KV_GROUPS + g), pipeline_mode=once_per_group),
            pl.BlockSpec((dh, s), lambda g, qb: (N_KV_GROUPS + g, 0), pipeline_mode=once_per_group),
            strip_spec(cmp_strip), strip_spec(sel_strip), strip_spec(win_strip),
            pl.BlockSpec((gates_t.shape[0], Q_TILE), lambda g, qb: (0, qb)),
            pl.BlockSpec((n_sel, n_cmp), lambda g, qb: (0, 0)),
        ],
        out_specs=pl.BlockSpec((Q_TILE, gw), lambda g, qb: (qb, g)),
        out_shape=jax.ShapeDtypeStruct((s, dq), jnp.bfloat16),
        scratch_shapes=[pltpu.VMEM((dh + AUG_ROWS, heads * Q_TILE), jnp.float32),
                        pltpu.VMEM((dh, heads * Q_TILE), jnp.float32),
                        pltpu.VMEM((n_cmp, heads * Q_TILE), jnp.bfloat16),
                        pltpu.VMEM((n_sel, heads * Q_TILE), jnp.float32),
                        pltpu.VMEM((KEY_TILE, heads * Q_TILE), jnp.float32),
                        pltpu.VMEM((KEY_TILE, heads * Q_TILE), jnp.float32)],
        compiler_params=_compiler_params(("arbitrary", "arbitrary")),
        name="nsa_attention",
    )(q_t, k_c, vt_c, k_sw, vt_sw, k_sw, vt_sw, cmp_strip, sel_strip, win_strip, gates_t,
      _pool_matrix_t(n_sel, n_cmp))


def _bf16(w):
    return w.astype(jnp.bfloat16)


def nsa_mixer(h, w_in, cmp_pos, cmp_w1, cmp_w2, w_out_all, layer, strips):
    dh, g = HEAD_DIM, N_KV_GROUPS
    q_end = w_out_all.shape[1]
    half = g * dh
    kv_end = q_end + N_BRANCHES * 2 * half
    n_gates = w_in.shape[1] - kv_end
    scale2 = dh ** -0.5 * LOG2_E

    def cols(branch, is_v):
        lo = q_end + branch * 2 * half + (half if is_v else 0)
        return w_in[:, lo:lo + half]

    q_t = matmul_nt(_bf16(w_in[:, :q_end].T), h, jnp.bfloat16, out_scale=scale2, name="nsa_q_proj")
    kv_cmp = matmul(h, _bf16(w_in[:, q_end:q_end + 2 * half]), jnp.float32, name="nsa_kv_cmp_proj")
    k_sw = matmul(h, _bf16(jnp.concatenate([cols(1, False), cols(2, False)], axis=1)), jnp.bfloat16,
                  name="nsa_k_proj")
    vt_sw = matmul_nt(_bf16(jnp.concatenate([cols(1, True), cols(2, True)], axis=1).T), h, jnp.bfloat16,
                      name="nsa_vt_proj")
    w_gate_t = jnp.pad(_bf16(w_in[:, kv_end:].T), ((0, V7X_LANES - n_gates), (0, 0)))
    gates_t = matmul_nt(w_gate_t, h, jnp.float32, sigmoid=True, name="nsa_gate_proj")

    kv_c = compress_kv(kv_cmp, cmp_pos, _bf16(cmp_w1), _bf16(cmp_w2))
    k_c = kv_c[:g]
    vt_c = jnp.swapaxes(kv_c[g:], 1, 2)
    o = nsa_attention(q_t, k_c, vt_c, k_sw, vt_sw, gates_t, strips)
    return matmul(o, w_out_all, jnp.float32, (layer,), name="nsa_out_proj")


def pool_mixer(h, w_in_all, w_group_all, scale, w_out_all, layer):
    y = pool_front(h, w_in_all, w_group_all, scale, layer)
    return matmul(y, w_out_all, jnp.float32, (layer,), name="pool_out_proj")


def kernel(x, norm_gains, ffn_w_gate, ffn_w_up, ffn_w_down, pool_w_in, pool_w_group, pool_scale, pool_w_out,
           nsa_w_in, nsa_cmp_pos, nsa_cmp_w1, nsa_cmp_w2, nsa_w_out, rel_bias):
    b, s, d = x.shape
    depth = norm_gains.shape[0]
    outs = []
    strips = nsa_strips(rel_bias) if depth > 1 else None
    wg, wu, wd = _bf16(ffn_w_gate), _bf16(ffn_w_up), _bf16(ffn_w_down)
    p_in, p_grp, p_out, n_out = _bf16(pool_w_in), _bf16(pool_w_group), _bf16(pool_w_out), _bf16(nsa_w_out)
    for bi in range(b):
        xs = x[bi]
        h = rms_norm_cast(xs, norm_gains[0, 0])
        for i in range(depth):
            gains = norm_gains[i]
            f1 = swiglu_ffn(h, wg, wu, wd, (i, 0))
            xs, h = residual_norm(xs, f1, gains[1], gains[2], 0.5)
            li = i // 2
            if i % 2 == 0:
                m = pool_mixer(h, p_in, p_grp, pool_scale[li], p_out, li)
            else:
                m = nsa_mixer(h, nsa_w_in[li], nsa_cmp_pos[li], nsa_cmp_w1[li], nsa_cmp_w2[li], n_out, li, strips)
            xs, h = residual_norm(xs, m, gains[3], gains[4], 1.0)
            f2 = swiglu_ffn(h, wg, wu, wd, (i, 1))
            g_next = norm_gains[i + 1, 0] if i + 1 < depth else None
            xs, h = residual_norm(xs, f2, gains[5], g_next, 0.5)
        outs.append(xs)
    return jnp.stack(outs, axis=0)
```

```python
import functools
import math

import jax
import jax.numpy as jnp
from jax import lax
from jax.experimental import pallas as pl
from jax.experimental.pallas import tpu as pltpu

NORM_EPS = 1e-6
POOL_WINDOWS = (2, 4, 8, 16)
HEAD_DIM = 128
N_KV_GROUPS = 4
N_BRANCHES = 3
CMP_BLOCK = 32
CMP_STRIDE = 16
SEL_BLOCK = 64
SEL_BLOCK_LOG2 = 6
N_SELECT = 16
WINDOW = 512
N_BUCKETS = 32
MAX_EXACT = N_BUCKETS // 2
NEG_INF = -1e30
BELOW_NEG_INF = -3e38
LOG2_E = math.log2(math.e)

V7X_LANES = 128
V7X_SUBLANES = 8
V7X_MXU_WIDTH = 256
V7X_VMEM_LIMIT_BYTES = 56 * 1024 * 1024

Q_TILE = 128
KEY_TILE = 512
SELECT_VARIANTS = 8
WIN_BLOCKS = WINDOW // Q_TILE + 1
AUG_ROWS = 16


def _compiler_params(semantics):
    return pltpu.CompilerParams(dimension_semantics=semantics, vmem_limit_bytes=V7X_VMEM_LIMIT_BYTES)


def _round_up(x, m):
    return -(-x // m) * m


def _rms(xf, gain):
    ms = jnp.mean(xf * xf, axis=-1, keepdims=True)
    return xf * lax.rsqrt(ms + NORM_EPS) * gain


_NT = (((1,), (1,)), ((), ()))


def _norm_body(x_ref, g_ref, h_ref):
    h_ref[...] = _rms(x_ref[...], g_ref[...]).astype(h_ref.dtype)


def rms_norm_cast(x, gain, *, rows=256):
    s, d = x.shape
    return pl.pallas_call(
        _norm_body,
        grid=(s // rows,),
        in_specs=[pl.BlockSpec((rows, d), lambda i: (i, 0)), pl.BlockSpec((1, d), lambda i: (0, 0))],
        out_specs=pl.BlockSpec((rows, d), lambda i: (i, 0)),
        out_shape=jax.ShapeDtypeStruct((s, d), jnp.bfloat16),
        compiler_params=_compiler_params(("arbitrary",)),
        name="rms_norm_cast",
    )(x, gain.reshape(1, d))


def _resnorm_body(x_ref, f_ref, gp_ref, gn_ref, xo_ref, h_ref, *, alpha):
    y = x_ref[...] + alpha * _rms(f_ref[...], gp_ref[...])
    xo_ref[...] = y
    h_ref[...] = _rms(y, gn_ref[...]).astype(h_ref.dtype)


def _resnorm_last_body(x_ref, f_ref, gp_ref, xo_ref, *, alpha):
    xo_ref[...] = x_ref[...] + alpha * _rms(f_ref[...], gp_ref[...])


def residual_norm(x, f, g_post, g_next, alpha, *, rows=256):
    s, d = x.shape
    row_spec = pl.BlockSpec((rows, d), lambda i: (i, 0))
    gain_spec = pl.BlockSpec((1, d), lambda i: (0, 0))
    if g_next is None:
        return pl.pallas_call(
            functools.partial(_resnorm_last_body, alpha=alpha),
            grid=(s // rows,),
            in_specs=[row_spec, row_spec, gain_spec],
            out_specs=row_spec,
            out_shape=jax.ShapeDtypeStruct((s, d), jnp.float32),
            compiler_params=_compiler_params(("arbitrary",)),
            name="residual_norm_last",
        )(x, f, g_post.reshape(1, d)), None
    return pl.pallas_call(
        functools.partial(_resnorm_body, alpha=alpha),
        grid=(s // rows,),
        in_specs=[row_spec, row_spec, gain_spec, gain_spec],
        out_specs=[row_spec, row_spec],
        out_shape=[jax.ShapeDtypeStruct((s, d), jnp.float32), jax.ShapeDtypeStruct((s, d), jnp.bfloat16)],
        compiler_params=_compiler_params(("arbitrary",)),
        name="residual_norm",
    )(x, f, g_post.reshape(1, d), g_next.reshape(1, d))


def _ffn_body(h_ref, wg_ref, wu_ref, wd_ref, o_ref, a_ref, *, n_up, tf):
    j = pl.program_id(1)

    @pl.when(j < n_up)
    def _():
        h = h_ref[...]
        g = jnp.dot(h, wg_ref[...], preferred_element_type=jnp.float32)
        u = jnp.dot(h, wu_ref[...], preferred_element_type=jnp.float32)
        a = (g * jax.nn.sigmoid(g)) * u
        a_ref[:, pl.ds(pl.multiple_of(j * tf, tf), tf)] = a.astype(a_ref.dtype)

    @pl.when(j >= n_up)
    def _():
        o_ref[...] = jnp.dot(a_ref[...], wd_ref[...], preferred_element_type=jnp.float32)


def swiglu_ffn(h, w_gate, w_up, w_down, index=(), *, tm=1024, tf=256, tn=256):
    s, d = h.shape
    f = w_gate.shape[-1]
    tm = min(tm, s)
    n_up, n_down = f // tf, d // tn
    lead = (None,) * len(index)
    return pl.pallas_call(
        functools.partial(_ffn_body, n_up=n_up, tf=tf),
        grid=(s // tm, n_up + n_down),
        in_specs=[
            pl.BlockSpec((tm, d), lambda i, j: (i, 0)),
            pl.BlockSpec(lead + (d, tf), lambda i, j: (*index, 0, jnp.minimum(j, n_up - 1))),
            pl.BlockSpec(lead + (d, tf), lambda i, j: (*index, 0, jnp.minimum(j, n_up - 1))),
            pl.BlockSpec(lead + (f, tn), lambda i, j: (*index, 0, jnp.maximum(j - n_up, 0))),
        ],
        out_specs=pl.BlockSpec((tm, tn), lambda i, j: (i, jnp.maximum(j - n_up, 0))),
        out_shape=jax.ShapeDtypeStruct((s, d), jnp.float32),
        scratch_shapes=[pltpu.VMEM((tm, f), jnp.bfloat16)],
        compiler_params=_compiler_params(("arbitrary", "arbitrary")),
        name="swiglu_ffn",
    )(h, w_gate, w_up, w_down)


def _matmul_body(a_ref, b_ref, o_ref):
    o_ref[...] = jnp.dot(a_ref[...], b_ref[...], preferred_element_type=jnp.float32).astype(o_ref.dtype)


def matmul(a, b, out_dtype, index=(), *, tm=1024, tn=512, name="matmul"):
    m, k = a.shape
    n = b.shape[-1]
    tm, tn = min(tm, m), min(tn, n)
    lead = (None,) * len(index)
    return pl.pallas_call(
        _matmul_body,
        grid=(m // tm, n // tn),
        in_specs=[pl.BlockSpec((tm, k), lambda i, j: (i, 0)),
                  pl.BlockSpec(lead + (k, tn), lambda i, j: (*index, 0, j))],
        out_specs=pl.BlockSpec((tm, tn), lambda i, j: (i, j)),
        out_shape=jax.ShapeDtypeStruct((m, n), out_dtype),
        compiler_params=_compiler_params(("arbitrary", "arbitrary")),
        name=name,
    )(a, b)


def _matmul_nt_body(bt_ref, a_ref, o_ref, *, sigmoid, out_scale):
    r = lax.dot_general(bt_ref[...], a_ref[...], _NT, preferred_element_type=jnp.float32)
    if sigmoid:
        r = jax.nn.sigmoid(r)
    if out_scale is not None:
        r = r * out_scale
    o_ref[...] = r.astype(o_ref.dtype)


def matmul_nt(bt, a, out_dtype, *, tm=1024, tn=512, sigmoid=False, out_scale=None, name="matmul_nt"):
    n, k = bt.shape
    m = a.shape[0]
    tm, tn = min(tm, m), min(tn, n)
    return pl.pallas_call(
        functools.partial(_matmul_nt_body, sigmoid=sigmoid, out_scale=out_scale),
        grid=(m // tm, n // tn),
        in_specs=[pl.BlockSpec((tn, k), lambda i, j: (j, 0)), pl.BlockSpec((tm, k), lambda i, j: (i, 0))],
        out_specs=pl.BlockSpec((tn, tm), lambda i, j: (j, i)),
        out_shape=jax.ShapeDtypeStruct((n, m), out_dtype),
        compiler_params=_compiler_params(("arbitrary", "arbitrary")),
        name=name,
    )(bt, a)


POOL_HALO = 16


def _pool_body(h_ref, win_ref, wgrp_ref, scale_ref, y_ref, halo_ref, *, tm):
    grp = pl.program_id(0)
    i = pl.program_id(1)
    u = jnp.dot(h_ref[...], win_ref[...], preferred_element_type=jnp.float32)

    @pl.when(i == 0)
    def _():
        halo_ref[...] = jnp.zeros_like(halo_ref)

    ext = jnp.concatenate([halo_ref[...], u], axis=0)
    halo_ref[...] = u[tm - POOL_HALO:, :]
    s2 = ext + pltpu.roll(ext, 1, 0)
    s4 = s2 + pltpu.roll(s2, 2, 0)
    s8 = s4 + pltpu.roll(s4, 4, 0)
    s16 = s8 + pltpu.roll(s8, 8, 0)
    sw = jnp.where(grp == 0, s2, jnp.where(grp == 1, s4, jnp.where(grp == 2, s8, s16)))[POOL_HALO:, :]
    w = jnp.where(grp == 0, POOL_WINDOWS[0], jnp.where(grp == 1, POOL_WINDOWS[1],
                  jnp.where(grp == 2, POOL_WINDOWS[2], POOL_WINDOWS[3])))
    t = i * tm + lax.broadcasted_iota(jnp.int32, (tm, 1), 0)
    count = jnp.minimum(t + 1, w).astype(jnp.float32)
    d = sw / count - u
    y = jnp.dot(d.astype(jnp.bfloat16), wgrp_ref[0], preferred_element_type=jnp.float32)
    y_ref[...] = (y * scale_ref[...]).astype(y_ref.dtype)


def pool_front(h, w_in, w_group, scale, layer, *, tm=512):
    s, d = h.shape
    _, n_groups, gw, _ = w_group.shape
    tm = min(tm, s)
    return pl.pallas_call(
        functools.partial(_pool_body, tm=tm),
        grid=(n_groups, s // tm),
        in_specs=[
            pl.BlockSpec((tm, d), lambda g, i: (i, 0)),
            pl.BlockSpec((None, d, gw), lambda g, i: (layer, 0, g)),
            pl.BlockSpec((None, 1, gw, gw), lambda g, i: (layer, g, 0, 0)),
            pl.BlockSpec((1, gw), lambda g, i: (0, g)),
        ],
        out_specs=pl.BlockSpec((tm, gw), lambda g, i: (i, g)),
        out_shape=jax.ShapeDtypeStruct((s, d), jnp.bfloat16),
        scratch_shapes=[pltpu.VMEM((POOL_HALO, gw), jnp.float32)],
        compiler_params=_compiler_params(("arbitrary", "arbitrary")),
        name="pool_front",
    )(h, w_in, w_group, scale.reshape(1, d))


def _bucket_thresholds():
    thr = list(range(1, MAX_EXACT + 1))
    for k in range(1, N_BUCKETS - MAX_EXACT):
        thr.append(math.isqrt(256 * 2 ** k - 1) + 1)
    return tuple(thr)


BUCKET_THRESHOLDS = _bucket_thresholds()
LAST_BUCKET_DIST = BUCKET_THRESHOLDS[-1]
STRIP_ROW_BLOCK = 256


def _strip_body(tab_ref, o_ref, *, rows, row_step, const, max_dist):
    h = pl.program_id(0)
    r = pl.program_id(1)
    a = r * rows + lax.broadcasted_iota(jnp.int32, (rows, V7X_LANES), 0)
    l = lax.broadcasted_iota(jnp.int32, (rows, V7X_LANES), 1)
    dist = l + const - row_step * a
    v = jnp.full((rows, V7X_LANES), tab_ref[0, h], jnp.float32)
    for b, thr in enumerate(BUCKET_THRESHOLDS, start=1):
        v = jnp.where(dist >= thr, tab_ref[b, h], v)
    ok = dist >= 0
    if max_dist is not None:
        ok = jnp.logical_and(ok, dist < max_dist)
    o_ref[0] = jnp.where(ok, v * LOG2_E, NEG_INF)


def bias_strip(rel_table, min_rows, *, row_step, const, max_dist):
    n_heads = rel_table.shape[1]
    rows = STRIP_ROW_BLOCK
    n_rows = _round_up(min_rows, rows)
    return pl.pallas_call(
        functools.partial(_strip_body, rows=rows, row_step=row_step, const=const, max_dist=max_dist),
        grid=(n_heads, n_rows // rows),
        in_specs=[pl.BlockSpec(memory_space=pltpu.SMEM)],
        out_specs=pl.BlockSpec((1, rows, V7X_LANES), lambda h, r: (h, r, 0)),
        out_shape=jax.ShapeDtypeStruct((n_heads, n_rows, V7X_LANES), jnp.float32),
        compiler_params=_compiler_params(("arbitrary", "arbitrary")),
        name="bias_strip",
    )(rel_table)


SEL_STRIP_CONST = _round_up(LAST_BUCKET_DIST + V7X_SUBLANES - 1, V7X_SUBLANES)
SEL_STRIP_ROWS = SEL_STRIP_CONST + KEY_TILE
WIN_STRIP_CONST = _round_up(WINDOW + V7X_SUBLANES - 1, V7X_SUBLANES)
WIN_STRIP_ROWS = WIN_STRIP_CONST + Q_TILE
CMP_DIST_OFFSET = CMP_BLOCK - 1
CMP_STRIP_CONST = _round_up(LAST_BUCKET_DIST + CMP_STRIDE * (V7X_SUBLANES - 1) + CMP_DIST_OFFSET,
                            CMP_STRIDE * V7X_SUBLANES) - CMP_DIST_OFFSET
CMP_STRIP_ROW0 = (CMP_STRIP_CONST + CMP_DIST_OFFSET) // CMP_STRIDE
CMP_STRIP_HI = _round_up((V7X_LANES + CMP_STRIP_CONST) // CMP_STRIDE + 1, V7X_SUBLANES)
CMP_STRIP_ROWS = CMP_STRIP_HI + V7X_SUBLANES


def nsa_strips(rel_bias):
    sel_strip = bias_strip(rel_bias, SEL_STRIP_ROWS, row_step=1, const=SEL_STRIP_CONST, max_dist=None)
    win_strip = bias_strip(rel_bias, WIN_STRIP_ROWS, row_step=1, const=WIN_STRIP_CONST, max_dist=WINDOW)
    cmp_strip = bias_strip(rel_bias, CMP_STRIP_ROWS, row_step=CMP_STRIDE, const=CMP_STRIP_CONST, max_dist=None)
    return sel_strip, win_strip, cmp_strip


def _bias_columns(strip_ref, heads, offsets):
    cols = [jnp.concatenate([strip_ref[h, pl.ds(e, V7X_SUBLANES), :] for e in offsets], axis=0) for h in heads]
    return jnp.concatenate(cols, axis=1) if len(cols) > 1 else cols[0]


def _compress_body(x_ref, pos_ref, w1a_ref, w1b_ref, w2_ref, o_ref, acca_ref, accb_ref, *, n_chunks):
    l = pl.program_id(1)

    @pl.when(l == 0)
    def _():
        acca_ref[...] = jnp.zeros_like(acca_ref)
        accb_ref[...] = jnp.zeros_like(accb_ref)

    x = x_ref[...]
    pa = pos_ref[0, pl.ds(l, 1), :]
    pb = pos_ref[0, pl.ds(l + CMP_STRIDE, 1), :]
    acca_ref[...] += jnp.dot((x + pa).astype(jnp.bfloat16), w1a_ref[0], preferred_element_type=jnp.float32)
    accb_ref[...] += jnp.dot((x + pb).astype(jnp.bfloat16), w1b_ref[0], preferred_element_type=jnp.float32)

    @pl.when(l == CMP_STRIDE - 1)
    def _():
        pre = acca_ref[...] + pltpu.roll(accb_ref[...], n_chunks - 1, 0)
        act = jax.nn.gelu(pre)
        o_ref[0] = jnp.dot(act.astype(jnp.bfloat16), w2_ref[0],
                           preferred_element_type=jnp.float32).astype(o_ref.dtype)


def compress_kv(kv_cmp, pos, w1, w2):
    s = kv_cmp.shape[0]
    dh = HEAD_DIM
    n_chunks = s // CMP_STRIDE
    n_kvg = 2 * N_KV_GROUPS
    x = kv_cmp.reshape(n_chunks, CMP_STRIDE * n_kvg * dh)
    w1r = w1.reshape(2, CMP_BLOCK, dh, dh)
    return pl.pallas_call(
        functools.partial(_compress_body, n_chunks=n_chunks),
        grid=(n_kvg, CMP_STRIDE),
        in_specs=[
            pl.BlockSpec((n_chunks, dh), lambda c, l: (0, l * n_kvg + c)),
            pl.BlockSpec((1, CMP_BLOCK, dh), lambda c, l: (c // N_KV_GROUPS, 0, 0)),
            pl.BlockSpec((1, None, dh, dh), lambda c, l: (c // N_KV_GROUPS, l, 0, 0)),
            pl.BlockSpec((1, None, dh, dh), lambda c, l: (c // N_KV_GROUPS, l + CMP_STRIDE, 0, 0)),
            pl.BlockSpec((1, dh, dh), lambda c, l: (c // N_KV_GROUPS, 0, 0)),
        ],
        out_specs=pl.BlockSpec((1, n_chunks, dh), lambda c, l: (c, 0, 0)),
        out_shape=jax.ShapeDtypeStruct((n_kvg, n_chunks, dh), jnp.bfloat16),
        scratch_shapes=[pltpu.VMEM((n_chunks, dh), jnp.float32), pltpu.VMEM((n_chunks, dh), jnp.float32)],
        compiler_params=_compiler_params(("arbitrary", "arbitrary")),
        name="compress_kv",
    )(x, pos, w1r, w1r, w2)


def _online_softmax_tile(s, vt, m, acc_ref):
    m_new = jnp.maximum(m, jnp.max(s, axis=0, keepdims=True))
    alpha = jnp.exp2(m - m_new)
    p = jnp.exp2(s - m_new).astype(jnp.bfloat16)
    acc_ref[...] = alpha * acc_ref[...] + jnp.dot(vt, p, preferred_element_type=jnp.float32)
    return m_new


def _ones_rows(keys):
    row = lax.broadcasted_iota(jnp.int32, (AUG_ROWS, keys), 0)
    return jnp.where(row == 0, 1.0, 0.0).astype(jnp.bfloat16)


def _gate_row(gt_ref, first_row, heads):
    rows = gt_ref[pl.ds(pl.multiple_of(first_row, V7X_SUBLANES), V7X_SUBLANES), :]
    return jnp.concatenate([rows[h:h + 1, :] for h in range(heads)], axis=1)


def _nsa_body(q_ref, kc_ref, vct_ref, ks_ref, vst_ref, kw_ref, vwt_ref,
              cstrip_ref, sstrip_ref, wstrip_ref, gt_ref, poolt_ref, o_ref,
              acc_ref, out_ref, p_ref, selm_ref, sa_ref, sb_ref, *, heads, n_cmp, n_sel):
    g = pl.program_id(0)
    qb = pl.program_id(1)
    t0 = qb * Q_TILE
    dh = HEAD_DIM
    lanes = heads * Q_TILE
    heads_per_chunk = V7X_MXU_WIDTH // Q_TILE
    qt = jnp.concatenate([q_ref[h * dh:(h + 1) * dh, :] for h in range(heads)], axis=1)
    gate_base = g * heads

    def compressed_and_select(n_blocks):
        n_rows = n_blocks * (SEL_BLOCK // CMP_STRIDE)
        kc = kc_ref[0, :n_rows, :]
        cmp_offsets = [pl.multiple_of(jnp.clip(CMP_STRIP_ROW0 - qb * (Q_TILE // CMP_STRIDE) + m0, 0, CMP_STRIP_HI),
                                      V7X_SUBLANES) for m0 in range(0, n_rows, V7X_SUBLANES)]
        imp = jnp.zeros((n_rows, Q_TILE), jnp.float32)
        for c in range(heads // heads_per_chunk):
            hs = range(c * heads_per_chunk, (c + 1) * heads_per_chunk)
            lo, hi = c * V7X_MXU_WIDTH, (c + 1) * V7X_MXU_WIDTH
            s = jnp.dot(kc, qt[:, lo:hi], preferred_element_type=jnp.float32)
            s = s + _bias_columns(cstrip_ref, hs, cmp_offsets)
            m = jnp.max(s, axis=0, keepdims=True)
            e = jnp.exp2(s - m)
            l = jnp.maximum(jnp.sum(e, axis=0, keepdims=True), 1e-30)
            p = e * jnp.where(m > 0.5 * NEG_INF, 1.0 / l, 0.0)
            for k in range(heads_per_chunk):
                imp = imp + p[:, k * Q_TILE:(k + 1) * Q_TILE]
            p_ref[:n_rows, lo:hi] = p.astype(p_ref.dtype)
        gate_c = _gate_row(gt_ref, gate_base, heads)
        out_ref[...] = jnp.dot(vct_ref[0, :, :n_rows], p_ref[:n_rows, :], preferred_element_type=jnp.float32) * gate_c

        poolt = poolt_ref[:n_blocks, :n_rows]
        hi_t = imp.astype(jnp.bfloat16)
        r1 = imp - hi_t.astype(jnp.float32)
        mid_t = r1.astype(jnp.bfloat16)
        lo_t = (r1 - mid_t.astype(jnp.float32)).astype(jnp.bfloat16)
        imp_sel = (jnp.dot(poolt, hi_t, preferred_element_type=jnp.float32)
                   + jnp.dot(poolt, mid_t, preferred_element_type=jnp.float32)
                   + jnp.dot(poolt, lo_t, preferred_element_type=jnp.float32))

        j = lax.broadcasted_iota(jnp.int32, (n_blocks, Q_TILE), 0)
        t = t0 + lax.broadcasted_iota(jnp.int32, (n_blocks, Q_TILE), 1)
        cur = jnp.right_shift(t, SEL_BLOCK_LOG2)
        forced = jnp.logical_or(j == 0, jnp.logical_or(j == cur, j == cur - 1))
        valid = j * SEL_BLOCK <= t
        n_forced = 3
        score = jnp.where(jnp.logical_and(valid, jnp.logical_not(forced)), imp_sel, NEG_INF)
        picked = jnp.where(forced, 1.0, 0.0)
        jf = j.astype(jnp.float32)
        for _ in range(min(N_SELECT, n_sel) - n_forced):
            best = jnp.max(score, axis=0, keepdims=True)
            first = jnp.min(jnp.where(score == best, jf, float(n_blocks)), axis=0, keepdims=True)
            hit = jf == first
            picked = jnp.where(hit, 1.0, picked)
            score = jnp.where(hit, BELOW_NEG_INF, score)
        selm_ref[:n_blocks, :] = jnp.concatenate([picked - 1.0] * heads, axis=1)
        if n_blocks < n_sel:
            selm_ref[n_blocks:, :] = jnp.full((n_sel - n_blocks, lanes), -1.0, jnp.float32)

    blocks_step = n_sel // SELECT_VARIANTS
    variant = (t0 + Q_TILE - 1) // (SEL_BLOCK * blocks_step)
    for v in range(SELECT_VARIANTS):
        pl.when(variant == v)(functools.partial(compressed_and_select, (v + 1) * blocks_step))

    all_heads = range(heads)
    win_offsets = []
    for back in range(WIN_BLOCKS):
        for i0 in range(0, Q_TILE, V7X_SUBLANES):
            e = WIN_STRIP_CONST - back * Q_TILE + i0
            win_offsets.append(e if back == 0 else pl.multiple_of(jnp.where(back > qb, 0, e), V7X_SUBLANES))
    win_starts = [pl.multiple_of(jnp.maximum(qb - back, 0) * Q_TILE, Q_TILE) for back in range(WIN_BLOCKS)]
    k_w = jnp.concatenate([kw_ref[pl.ds(st, Q_TILE), :] for st in win_starts], axis=0)
    vt_w = jnp.concatenate([jnp.concatenate([vwt_ref[:, pl.ds(st, Q_TILE)] for st in win_starts], axis=1),
                            _ones_rows(WIN_BLOCKS * Q_TILE)], axis=0)
    s_w = jnp.dot(k_w, qt, preferred_element_type=jnp.float32) + _bias_columns(wstrip_ref, all_heads, win_offsets)
    p_w = jnp.exp2(s_w - jnp.max(s_w, axis=0, keepdims=True)).astype(jnp.bfloat16)
    acc_w = jnp.dot(vt_w, p_w, preferred_element_type=jnp.float32)
    gate_w = _gate_row(gt_ref, 2 * N_KV_GROUPS * heads + gate_base, heads)
    out_ref[...] += acc_w[:dh, :] * (gate_w / acc_w[dh:dh + 1, :])

    blocks_per_tile = KEY_TILE // SEL_BLOCK
    key_blk = jnp.right_shift(lax.broadcasted_iota(jnp.int32, (KEY_TILE, V7X_LANES), 0), SEL_BLOCK_LOG2)
    blk_col = lax.broadcasted_iota(jnp.int32, (KEY_TILE, V7X_LANES), 1)
    key_mask_cols = jnp.where(key_blk == blk_col, -NEG_INF, 0.0).astype(jnp.bfloat16)
    zero_rows = jnp.zeros((V7X_LANES - AUG_ROWS, lanes), jnp.bfloat16)
    sel_ones = _ones_rows(KEY_TILE)

    last_tile = (t0 + Q_TILE - 1) // KEY_TILE

    def scores(kt):
        start = pl.multiple_of(kt * KEY_TILE, KEY_TILE)
        k_aug = jnp.concatenate([ks_ref[pl.ds(start, KEY_TILE), :], key_mask_cols], axis=1)
        sel_rows = selm_ref[pl.ds(pl.multiple_of(kt * blocks_per_tile, blocks_per_tile), blocks_per_tile), :]
        sel_rows = jnp.concatenate([sel_rows, jnp.zeros((AUG_ROWS - blocks_per_tile, lanes), jnp.float32)], axis=0)
        qt_aug = jnp.concatenate([qt, sel_rows.astype(jnp.bfloat16), zero_rows], axis=0)
        row0 = SEL_STRIP_CONST - (t0 - kt * KEY_TILE)
        offsets = [pl.multiple_of(jnp.maximum(row0 + i0, 0), V7X_SUBLANES) for i0 in range(0, KEY_TILE, V7X_SUBLANES)]
        s = jnp.dot(k_aug, qt_aug, preferred_element_type=jnp.float32)
        return s + _bias_columns(sstrip_ref, all_heads, offsets)

    def values(kt):
        start = pl.multiple_of(kt * KEY_TILE, KEY_TILE)
        return jnp.concatenate([vst_ref[:, pl.ds(start, KEY_TILE)], sel_ones], axis=0)

    def sel_step(i, m):
        kt = 2 * i
        sb_ref[...] = scores(kt + 1)
        m = _online_softmax_tile(sa_ref[...], values(kt), m, acc_ref)
        sa_ref[...] = scores(kt + 2)
        return _online_softmax_tile(sb_ref[...], values(kt + 1), m, acc_ref)

    acc_ref[...] = jnp.zeros_like(acc_ref)
    sa_ref[...] = scores(0)
    m_run = lax.fori_loop(0, last_tile // 2, sel_step, jnp.full((1, lanes), NEG_INF, jnp.float32))
    tail = 2 * (last_tile // 2)

    @pl.when(tail < last_tile)
    def _():
        sb_ref[...] = scores(tail + 1)
        m = _online_softmax_tile(sa_ref[...], values(tail), m_run, acc_ref)
        _online_softmax_tile(sb_ref[...], values(tail + 1), m, acc_ref)

    @pl.when(tail == last_tile)
    def _():
        _online_softmax_tile(sa_ref[...], values(tail), m_run, acc_ref)

    gate_s = _gate_row(gt_ref, N_KV_GROUPS * heads + gate_base, heads)
    out_t = out_ref[...] + acc_ref[:dh, :] * (gate_s / acc_ref[dh:dh + 1, :])
    for h in range(heads):
        o_ref[:, h * dh:(h + 1) * dh] = out_t[:, h * Q_TILE:(h + 1) * Q_TILE].T.astype(o_ref.dtype)


def _pool_matrix_t(n_sel, n_cmp):
    r = SEL_BLOCK // CMP_STRIDE
    c = CMP_BLOCK // CMP_STRIDE
    j = jnp.arange(n_sel)[:, None]
    m = jnp.arange(n_cmp)[None, :]
    lo = r * j - (c - 1)
    return jnp.logical_and(m >= lo, m <= lo + r + c - 2).astype(jnp.bfloat16)


def nsa_attention(q_t, k_c, vt_c, k_sw, vt_sw, gates_t, strips):
    dq, s = q_t.shape
    dh = HEAD_DIM
    heads = dq // dh // N_KV_GROUPS
    assert heads == V7X_SUBLANES, "gate rows of one group must fill one sublane tile"
    assert KEY_TILE // SEL_BLOCK == V7X_SUBLANES, "one key tile's selection rows must fill one sublane tile"
    gw = heads * dh
    n_cmp = k_c.shape[1]
    n_sel = s // SEL_BLOCK
    sel_strip, win_strip, cmp_strip = strips
    once_per_group = pl.Buffered(1)

    def strip_spec(strip):
        return pl.BlockSpec((heads, strip.shape[1], V7X_LANES), lambda g, qb: (g, 0, 0), pipeline_mode=once_per_group)

    return pl.pallas_call(
        functools.partial(_nsa_body, heads=heads, n_cmp=n_cmp, n_sel=n_sel),
        grid=(N_KV_GROUPS, s // Q_TILE),
        in_specs=[
            pl.BlockSpec((gw, Q_TILE), lambda g, qb: (g, qb)),
            pl.BlockSpec((1, n_cmp, dh), lambda g, qb: (g, 0, 0)),
            pl.BlockSpec((1, dh, n_cmp), lambda g, qb: (g, 0, 0)),
            pl.BlockSpec((s, dh), lambda g, qb: (0, g), pipeline_mode=once_per_group),
            pl.BlockSpec((dh, s), lambda g, qb: (g, 0), pipeline_mode=once_per_group),
            pl.BlockSpec((s, dh), lambda g, qb: (0, N_KV_GROUPS + g), pipeline_mode=once_per_group),
            pl.BlockSpec((dh, s), lambda g, qb: (N_KV_GROUPS + g, 0), pipeline_mode=once_per_group),
            strip_spec(cmp_strip), strip_spec(sel_strip), strip_spec(win_strip),
            pl.BlockSpec((gates_t.shape[0], Q_TILE), lambda g, qb: (0, qb)),
            pl.BlockSpec((n_sel, n_cmp), lambda g, qb: (0, 0)),
        ],
        out_specs=pl.BlockSpec((Q_TILE, gw), lambda g, qb: (qb, g)),
        out_shape=jax.ShapeDtypeStruct((s, dq), jnp.bfloat16),
        scratch_shapes=[pltpu.VMEM((dh + AUG_ROWS, heads * Q_TILE), jnp.float32),
                        pltpu.VMEM((dh, heads * Q_TILE), jnp.float32),
                        pltpu.VMEM((n_cmp, heads * Q_TILE), jnp.bfloat16),
                        pltpu.VMEM((n_sel, heads * Q_TILE), jnp.float32),
                        pltpu.VMEM((KEY_TILE, heads * Q_TILE), jnp.float32),
                        pltpu.VMEM((KEY_TILE, heads * Q_TILE), jnp.float32)],
        compiler_params=_compiler_params(("arbitrary", "arbitrary")),
        name="nsa_attention",
    )(q_t, k_c, vt_c, k_sw, vt_sw, k_sw, vt_sw, cmp_strip, sel_strip, win_strip, gates_t,
      _pool_matrix_t(n_sel, n_cmp))


def _bf16(w):
    return w.astype(jnp.bfloat16)


def nsa_mixer(h, w_in, cmp_pos, cmp_w1, cmp_w2, w_out_all, layer, strips):
    dh, g = HEAD_DIM, N_KV_GROUPS
    q_end = w_out_all.shape[1]
    half = g * dh
    kv_end = q_end + N_BRANCHES * 2 * half
    n_gates = w_in.shape[1] - kv_end
    scale2 = dh ** -0.5 * LOG2_E

    def cols(branch, is_v):
        lo = q_end + branch * 2 * half + (half if is_v else 0)
        return w_in[:, lo:lo + half]

    q_t = matmul_nt(_bf16(w_in[:, :q_end].T), h, jnp.bfloat16, out_scale=scale2, name="nsa_q_proj")
    kv_cmp = matmul(h, _bf16(w_in[:, q_end:q_end + 2 * half]), jnp.float32, name="nsa_kv_cmp_proj")
    k_sw = matmul(h, _bf16(jnp.concatenate([cols(1, False), cols(2, False)], axis=1)), jnp.bfloat16,
                  name="nsa_k_proj")
    vt_sw = matmul_nt(_bf16(jnp.concatenate([cols(1, True), cols(2, True)], axis=1).T), h, jnp.bfloat16,
                      name="nsa_vt_proj")
    w_gate_t = jnp.pad(_bf16(w_in[:, kv_end:].T), ((0, V7X_LANES - n_gates), (0, 0)))
    gates_t = matmul_nt(w_gate_t, h, jnp.float32, sigmoid=True, name="nsa_gate_proj")

    kv_c = compress_kv(kv_cmp, cmp_pos, _bf16(cmp_w1), _bf16(cmp_w2))
    k_c = kv_c[:g]
    vt_c = jnp.swapaxes(kv_c[g:], 1, 2)
    o = nsa_attention(q_t, k_c, vt_c, k_sw, vt_sw, gates_t, strips)
    return matmul(o, w_out_all, jnp.float32, (layer,), name="nsa_out_proj")


def pool_mixer(h, w_in_all, w_group_all, scale, w_out_all, layer):
    y = pool_front(h, w_in_all, w_group_all, scale, layer)
    return matmul(y, w_out_all, jnp.float32, (layer,), name="pool_out_proj")


def kernel(x, norm_gains, ffn_w_gate, ffn_w_up, ffn_w_down, pool_w_in, pool_w_group, pool_scale, pool_w_out,
           nsa_w_in, nsa_cmp_pos, nsa_cmp_w1, nsa_cmp_w2, nsa_w_out, rel_bias):
    b, s, d = x.shape
    depth = norm_gains.shape[0]
    outs = []
    strips = nsa_strips(rel_bias) if depth > 1 else None
    wg, wu, wd = _bf16(ffn_w_gate), _bf16(ffn_w_up), _bf16(ffn_w_down)
    p_in, p_grp, p_out, n_out = _bf16(pool_w_in), _bf16(pool_w_group), _bf16(pool_w_out), _bf16(nsa_w_out)
    for bi in range(b):
        xs = x[bi]
        h = rms_norm_cast(xs, norm_gains[0, 0])
        for i in range(depth):
            gains = norm_gains[i]
            f1 = swiglu_ffn(h, wg, wu, wd, (i, 0))
            xs, h = residual_norm(xs, f1, gains[1], gains[2], 0.5)
            li = i // 2
            if i % 2 == 0:
                m = pool_mixer(h, p_in, p_grp, pool_scale[li], p_out, li)
            else:
                m = nsa_mixer(h, nsa_w_in[li], nsa_cmp_pos[li], nsa_cmp_w1[li], nsa_cmp_w2[li], n_out, li, strips)
            xs, h = residual_norm(xs, m, gains[3], gains[4], 1.0)
            f2 = swiglu_ffn(h, wg, wu, wd, (i, 1))
            g_next = norm_gains[i + 1, 0] if i + 1 < depth else None
            xs, h = residual_norm(xs, f2, gains[5], g_next, 0.5)
        outs.append(xs)
    return jnp.stack(outs, axis=0)
```

```python
import functools
import math

import jax
import jax.numpy as jnp
from jax import lax
from jax.experimental import pallas as pl
from jax.experimental.pallas import tpu as pltpu

NORM_EPS = 1e-6
POOL_WINDOWS = (2, 4, 8, 16)
HEAD_DIM = 128
N_KV_GROUPS = 4
N_BRANCHES = 3
CMP_BLOCK = 32
CMP_STRIDE = 16
SEL_BLOCK = 64
SEL_BLOCK_LOG2 = 6
N_SELECT = 16
WINDOW = 512
N_BUCKETS = 32
MAX_EXACT = N_BUCKETS // 2
NEG_INF = -1e30
BELOW_NEG_INF = -3e38
LOG2_E = math.log2(math.e)

V7X_LANES = 128
V7X_SUBLANES = 8
V7X_MXU_WIDTH = 256
V7X_VMEM_LIMIT_BYTES = 56 * 1024 * 1024

Q_TILE = 128
KEY_TILE = 512
SELECT_VARIANTS = 8
WIN_BLOCKS = WINDOW // Q_TILE + 1
AUG_ROWS = 16


def _compiler_params(semantics):
    return pltpu.CompilerParams(dimension_semantics=semantics, vmem_limit_bytes=V7X_VMEM_LIMIT_BYTES)


def _round_up(x, m):
    return -(-x // m) * m


def _rms(xf, gain):
    ms = jnp.mean(xf * xf, axis=-1, keepdims=True)
    return xf * lax.rsqrt(ms + NORM_EPS) * gain


_NT = (((1,), (1,)), ((), ()))


def _norm_body(x_ref, g_ref, h_ref):
    h_ref[...] = _rms(x_ref[...], g_ref[...]).astype(h_ref.dtype)


def rms_norm_cast(x, gain, *, rows=256):
    s, d = x.shape
    return pl.pallas_call(
        _norm_body,
        grid=(s // rows,),
        in_specs=[pl.BlockSpec((rows, d), lambda i: (i, 0)), pl.BlockSpec((1, d), lambda i: (0, 0))],
        out_specs=pl.BlockSpec((rows, d), lambda i: (i, 0)),
        out_shape=jax.ShapeDtypeStruct((s, d), jnp.bfloat16),
        compiler_params=_compiler_params(("arbitrary",)),
        name="rms_norm_cast",
    )(x, gain.reshape(1, d))


def _resnorm_body(x_ref, f_ref, gp_ref, gn_ref, xo_ref, h_ref, *, alpha):
    y = x_ref[...] + alpha * _rms(f_ref[...], gp_ref[...])
    xo_ref[...] = y
    h_ref[...] = _rms(y, gn_ref[...]).astype(h_ref.dtype)


def _resnorm_last_body(x_ref, f_ref, gp_ref, xo_ref, *, alpha):
    xo_ref[...] = x_ref[...] + alpha * _rms(f_ref[...], gp_ref[...])


def residual_norm(x, f, g_post, g_next, alpha, *, rows=256):
    s, d = x.shape
    row_spec = pl.BlockSpec((rows, d), lambda i: (i, 0))
    gain_spec = pl.BlockSpec((1, d), lambda i: (0, 0))
    if g_next is None:
        return pl.pallas_call(
            functools.partial(_resnorm_last_body, alpha=alpha),
            grid=(s // rows,),
            in_specs=[row_spec, row_spec, gain_spec],
            out_specs=row_spec,
            out_shape=jax.ShapeDtypeStruct((s, d), jnp.float32),
            compiler_params=_compiler_params(("arbitrary",)),
            name="residual_norm_last",
        )(x, f, g_post.reshape(1, d)), None
    return pl.pallas_call(
        functools.partial(_resnorm_body, alpha=alpha),
        grid=(s // rows,),
        in_specs=[row_spec, row_spec, gain_spec, gain_spec],
        out_specs=[row_spec, row_spec],
        out_shape=[jax.ShapeDtypeStruct((s, d), jnp.float32), jax.ShapeDtypeStruct((s, d), jnp.bfloat16)],
        compiler_params=_compiler_params(("arbitrary",)),
        name="residual_norm",
    )(x, f, g_post.reshape(1, d), g_next.reshape(1, d))


def _ffn_body(h_ref, wg_ref, wu_ref, wd_ref, o_ref, a_ref, *, n_up, tf):
    j = pl.program_id(1)

    @pl.when(j < n_up)
    def _():
        h = h_ref[...]
        g = jnp.dot(h, wg_ref[...], preferred_element_type=jnp.float32)
        u = jnp.dot(h, wu_ref[...], preferred_element_type=jnp.float32)
        a = (g * jax.nn.sigmoid(g)) * u
        a_ref[:, pl.ds(pl.multiple_of(j * tf, tf), tf)] = a.astype(a_ref.dtype)

    @pl.when(j >= n_up)
    def _():
        o_ref[...] = jnp.dot(a_ref[...], wd_ref[...], preferred_element_type=jnp.float32)


def swiglu_ffn(h, w_gate, w_up, w_down, index=(), *, tm=1024, tf=512, tn=256):
    s, d = h.shape
    f = w_gate.shape[-1]
    tm = min(tm, s)
    n_up, n_down = f // tf, d // tn
    lead = (None,) * len(index)
    return pl.pallas_call(
        functools.partial(_ffn_body, n_up=n_up, tf=tf),
        grid=(s // tm, n_up + n_down),
        in_specs=[
            pl.BlockSpec((tm, d), lambda i, j: (i, 0), pipeline_mode=pl.Buffered(1)),
            pl.BlockSpec(lead + (d, tf), lambda i, j: (*index, 0, jnp.minimum(j, n_up - 1))),
            pl.BlockSpec(lead + (d, tf), lambda i, j: (*index, 0, jnp.minimum(j, n_up - 1))),
            pl.BlockSpec(lead + (f, tn), lambda i, j: (*index, 0, jnp.maximum(j - n_up, 0))),
        ],
        out_specs=pl.BlockSpec((tm, tn), lambda i, j: (i, jnp.maximum(j - n_up, 0))),
        out_shape=jax.ShapeDtypeStruct((s, d), jnp.float32),
        scratch_shapes=[pltpu.VMEM((tm, f), jnp.bfloat16)],
        compiler_params=_compiler_params(("arbitrary", "arbitrary")),
        name="swiglu_ffn",
    )(h, w_gate, w_up, w_down)


def _matmul_body(a_ref, b_ref, o_ref):
    o_ref[...] = jnp.dot(a_ref[...], b_ref[...], preferred_element_type=jnp.float32).astype(o_ref.dtype)


def matmul(a, b, out_dtype, index=(), *, tm=1024, tn=512, name="matmul"):
    m, k = a.shape
    n = b.shape[-1]
    tm, tn = min(tm, m), min(tn, n)
    lead = (None,) * len(index)
    return pl.pallas_call(
        _matmul_body,
        grid=(m // tm, n // tn),
        in_specs=[pl.BlockSpec((tm, k), lambda i, j: (i, 0)),
                  pl.BlockSpec(lead + (k, tn), lambda i, j: (*index, 0, j))],
        out_specs=pl.BlockSpec((tm, tn), lambda i, j: (i, j)),
        out_shape=jax.ShapeDtypeStruct((m, n), out_dtype),
        compiler_params=_compiler_params(("arbitrary", "arbitrary")),
        name=name,
    )(a, b)


def _matmul_nt_body(bt_ref, a_ref, o_ref, *, sigmoid, out_scale):
    r = lax.dot_general(bt_ref[...], a_ref[...], _NT, preferred_element_type=jnp.float32)
    if sigmoid:
        r = jax.nn.sigmoid(r)
    if out_scale is not None:
        r = r * out_scale
    o_ref[...] = r.astype(o_ref.dtype)


def matmul_nt(bt, a, out_dtype, *, tm=1024, tn=512, sigmoid=False, out_scale=None, name="matmul_nt"):
    n, k = bt.shape
    m = a.shape[0]
    tm, tn = min(tm, m), min(tn, n)
    return pl.pallas_call(
        functools.partial(_matmul_nt_body, sigmoid=sigmoid, out_scale=out_scale),
        grid=(m // tm, n // tn),
        in_specs=[pl.BlockSpec((tn, k), lambda i, j: (j, 0)), pl.BlockSpec((tm, k), lambda i, j: (i, 0))],
        out_specs=pl.BlockSpec((tn, tm), lambda i, j: (j, i)),
        out_shape=jax.ShapeDtypeStruct((n, m), out_dtype),
        compiler_params=_compiler_params(("arbitrary", "arbitrary")),
        name=name,
    )(bt, a)


POOL_HALO = 16


def _pool_body(h_ref, win_ref, wgrp_ref, scale_ref, y_ref, halo_ref, *, tm):
    grp = pl.program_id(0)
    i = pl.program_id(1)
    u = jnp.dot(h_ref[...], win_ref[...], preferred_element_type=jnp.float32)

    @pl.when(i == 0)
    def _():
        halo_ref[...] = jnp.zeros_like(halo_ref)

    ext = jnp.concatenate([halo_ref[...], u], axis=0)
    halo_ref[...] = u[tm - POOL_HALO:, :]
    s2 = ext + pltpu.roll(ext, 1, 0)
    s4 = s2 + pltpu.roll(s2, 2, 0)
    s8 = s4 + pltpu.roll(s4, 4, 0)
    s16 = s8 + pltpu.roll(s8, 8, 0)
    sw = jnp.where(grp == 0, s2, jnp.where(grp == 1, s4, jnp.where(grp == 2, s8, s16)))[POOL_HALO:, :]
    w = jnp.where(grp == 0, POOL_WINDOWS[0], jnp.where(grp == 1, POOL_WINDOWS[1],
                  jnp.where(grp == 2, POOL_WINDOWS[2], POOL_WINDOWS[3])))
    t = i * tm + lax.broadcasted_iota(jnp.int32, (tm, 1), 0)
    count = jnp.minimum(t + 1, w).astype(jnp.float32)
    d = sw / count - u
    y = jnp.dot(d.astype(jnp.bfloat16), wgrp_ref[0], preferred_element_type=jnp.float32)
    y_ref[...] = (y * scale_ref[...]).astype(y_ref.dtype)


def pool_front(h, w_in, w_group, scale, layer, *, tm=512):
    s, d = h.shape
    _, n_groups, gw, _ = w_group.shape
    tm = min(tm, s)
    return pl.pallas_call(
        functools.partial(_pool_body, tm=tm),
        grid=(n_groups, s // tm),
        in_specs=[
            pl.BlockSpec((tm, d), lambda g, i: (i, 0)),
            pl.BlockSpec((None, d, gw), lambda g, i: (layer, 0, g)),
            pl.BlockSpec((None, 1, gw, gw), lambda g, i: (layer, g, 0, 0)),
            pl.BlockSpec((1, gw), lambda g, i: (0, g)),
        ],
        out_specs=pl.BlockSpec((tm, gw), lambda g, i: (i, g)),
        out_shape=jax.ShapeDtypeStruct((s, d), jnp.bfloat16),
        scratch_shapes=[pltpu.VMEM((POOL_HALO, gw), jnp.float32)],
        compiler_params=_compiler_params(("arbitrary", "arbitrary")),
        name="pool_front",
    )(h, w_in, w_group, scale.reshape(1, d))


def _bucket_thresholds():
    thr = list(range(1, MAX_EXACT + 1))
    for k in range(1, N_BUCKETS - MAX_EXACT):
        thr.append(math.isqrt(256 * 2 ** k - 1) + 1)
    return tuple(thr)


BUCKET_THRESHOLDS = _bucket_thresholds()
LAST_BUCKET_DIST = BUCKET_THRESHOLDS[-1]
STRIP_ROW_BLOCK = 256


def _strip_body(tab_ref, o_ref, *, rows, row_step, const, max_dist):
    h = pl.program_id(0)
    r = pl.program_id(1)
    a = r * rows + lax.broadcasted_iota(jnp.int32, (rows, V7X_LANES), 0)
    l = lax.broadcasted_iota(jnp.int32, (rows, V7X_LANES), 1)
    dist = l + const - row_step * a
    v = jnp.full((rows, V7X_LANES), tab_ref[0, h], jnp.float32)
    for b, thr in enumerate(BUCKET_THRESHOLDS, start=1):
        v = jnp.where(dist >= thr, tab_ref[b, h], v)
    ok = dist >= 0
    if max_dist is not None:
        ok = jnp.logical_and(ok, dist < max_dist)
    o_ref[0] = jnp.where(ok, v * LOG2_E, NEG_INF)


def bias_strip(rel_table, min_rows, *, row_step, const, max_dist):
    n_heads = rel_table.shape[1]
    rows = STRIP_ROW_BLOCK
    n_rows = _round_up(min_rows, rows)
    return pl.pallas_call(
        functools.partial(_strip_body, rows=rows, row_step=row_step, const=const, max_dist=max_dist),
        grid=(n_heads, n_rows // rows),
        in_specs=[pl.BlockSpec(memory_space=pltpu.SMEM)],
        out_specs=pl.BlockSpec((1, rows, V7X_LANES), lambda h, r: (h, r, 0)),
        out_shape=jax.ShapeDtypeStruct((n_heads, n_rows, V7X_LANES), jnp.float32),
        compiler_params=_compiler_params(("arbitrary", "arbitrary")),
        name="bias_strip",
    )(rel_table)


SEL_STRIP_CONST = _round_up(LAST_BUCKET_DIST + V7X_SUBLANES - 1, V7X_SUBLANES)
SEL_STRIP_ROWS = SEL_STRIP_CONST + KEY_TILE
WIN_STRIP_CONST = _round_up(WINDOW + V7X_SUBLANES - 1, V7X_SUBLANES)
WIN_STRIP_ROWS = WIN_STRIP_CONST + Q_TILE
CMP_DIST_OFFSET = CMP_BLOCK - 1
CMP_STRIP_CONST = _round_up(LAST_BUCKET_DIST + CMP_STRIDE * (V7X_SUBLANES - 1) + CMP_DIST_OFFSET,
                            CMP_STRIDE * V7X_SUBLANES) - CMP_DIST_OFFSET
CMP_STRIP_ROW0 = (CMP_STRIP_CONST + CMP_DIST_OFFSET) // CMP_STRIDE
CMP_STRIP_HI = _round_up((V7X_LANES + CMP_STRIP_CONST) // CMP_STRIDE + 1, V7X_SUBLANES)
CMP_STRIP_ROWS = CMP_STRIP_HI + V7X_SUBLANES


def nsa_strips(rel_bias):
    sel_strip = bias_strip(rel_bias, SEL_STRIP_ROWS, row_step=1, const=SEL_STRIP_CONST, max_dist=None)
    win_strip = bias_strip(rel_bias, WIN_STRIP_ROWS, row_step=1, const=WIN_STRIP_CONST, max_dist=WINDOW)
    cmp_strip = bias_strip(rel_bias, CMP_STRIP_ROWS, row_step=CMP_STRIDE, const=CMP_STRIP_CONST, max_dist=None)
    return sel_strip, win_strip, cmp_strip


def _bias_columns(strip_ref, heads, offsets):
    cols = [jnp.concatenate([strip_ref[h, pl.ds(e, V7X_SUBLANES), :] for e in offsets], axis=0) for h in heads]
    return jnp.concatenate(cols, axis=1) if len(cols) > 1 else cols[0]


def _compress_body(x_ref, pos_ref, w1a_ref, w1b_ref, w2_ref, o_ref, acca_ref, accb_ref, *, n_chunks):
    l = pl.program_id(1)

    @pl.when(l == 0)
    def _():
        acca_ref[...] = jnp.zeros_like(acca_ref)
        accb_ref[...] = jnp.zeros_like(accb_ref)

    x = x_ref[...]
    pa = pos_ref[0, pl.ds(l, 1), :]
    pb = pos_ref[0, pl.ds(l + CMP_STRIDE, 1), :]
    acca_ref[...] += jnp.dot((x + pa).astype(jnp.bfloat16), w1a_ref[0], preferred_element_type=jnp.float32)
    accb_ref[...] += jnp.dot((x + pb).astype(jnp.bfloat16), w1b_ref[0], preferred_element_type=jnp.float32)

    @pl.when(l == CMP_STRIDE - 1)
    def _():
        pre = acca_ref[...] + pltpu.roll(accb_ref[...], n_chunks - 1, 0)
        act = jax.nn.gelu(pre)
        o_ref[0] = jnp.dot(act.astype(jnp.bfloat16), w2_ref[0],
                           preferred_element_type=jnp.float32).astype(o_ref.dtype)


def compress_kv(kv_cmp, pos, w1, w2):
    s = kv_cmp.shape[0]
    dh = HEAD_DIM
    n_chunks = s // CMP_STRIDE
    n_kvg = 2 * N_KV_GROUPS
    x = kv_cmp.reshape(n_chunks, CMP_STRIDE * n_kvg * dh)
    w1r = w1.reshape(2, CMP_BLOCK, dh, dh)
    return pl.pallas_call(
        functools.partial(_compress_body, n_chunks=n_chunks),
        grid=(n_kvg, CMP_STRIDE),
        in_specs=[
            pl.BlockSpec((n_chunks, dh), lambda c, l: (0, l * n_kvg + c)),
            pl.BlockSpec((1, CMP_BLOCK, dh), lambda c, l: (c // N_KV_GROUPS, 0, 0)),
            pl.BlockSpec((1, None, dh, dh), lambda c, l: (c // N_KV_GROUPS, l, 0, 0)),
            pl.BlockSpec((1, None, dh, dh), lambda c, l: (c // N_KV_GROUPS, l + CMP_STRIDE, 0, 0)),
            pl.BlockSpec((1, dh, dh), lambda c, l: (c // N_KV_GROUPS, 0, 0)),
        ],
        out_specs=pl.BlockSpec((1, n_chunks, dh), lambda c, l: (c, 0, 0)),
        out_shape=jax.ShapeDtypeStruct((n_kvg, n_chunks, dh), jnp.bfloat16),
        scratch_shapes=[pltpu.VMEM((n_chunks, dh), jnp.float32), pltpu.VMEM((n_chunks, dh), jnp.float32)],
        compiler_params=_compiler_params(("arbitrary", "arbitrary")),
        name="compress_kv",
    )(x, pos, w1r, w1r, w2)


def _online_softmax_tile(s, vt, m, acc_ref):
    m_new = jnp.maximum(m, jnp.max(s, axis=0, keepdims=True))
    alpha = jnp.exp2(m - m_new)
    p = jnp.exp2(s - m_new).astype(jnp.bfloat16)
    acc_ref[...] = alpha * acc_ref[...] + jnp.dot(vt, p, preferred_element_type=jnp.float32)
    return m_new


def _ones_rows(keys):
    row = lax.broadcasted_iota(jnp.int32, (AUG_ROWS, keys), 0)
    return jnp.where(row == 0, 1.0, 0.0).astype(jnp.bfloat16)


def _gate_row(gt_ref, first_row, heads):
    rows = gt_ref[pl.ds(pl.multiple_of(first_row, V7X_SUBLANES), V7X_SUBLANES), :]
    return jnp.concatenate([rows[h:h + 1, :] for h in range(heads)], axis=1)


def _nsa_body(q_ref, kc_ref, vct_ref, ks_ref, vst_ref, kw_ref, vwt_ref,
              cstrip_ref, sstrip_ref, wstrip_ref, gt_ref, poolt_ref, o_ref,
              acc_ref, out_ref, p_ref, selm_ref, sa_ref, sb_ref, *, heads, n_cmp, n_sel):
    g = pl.program_id(0)
    qb = pl.program_id(1)
    t0 = qb * Q_TILE
    dh = HEAD_DIM
    lanes = heads * Q_TILE
    heads_per_chunk = V7X_MXU_WIDTH // Q_TILE
    qt = jnp.concatenate([q_ref[h * dh:(h + 1) * dh, :] for h in range(heads)], axis=1)
    gate_base = g * heads

    def compressed_and_select(n_blocks):
        n_rows = n_blocks * (SEL_BLOCK // CMP_STRIDE)
        kc = kc_ref[0, :n_rows, :]
        cmp_offsets = [pl.multiple_of(jnp.clip(CMP_STRIP_ROW0 - qb * (Q_TILE // CMP_STRIDE) + m0, 0, CMP_STRIP_HI),
                                      V7X_SUBLANES) for m0 in range(0, n_rows, V7X_SUBLANES)]
        imp = jnp.zeros((n_rows, Q_TILE), jnp.float32)
        for c in range(heads // heads_per_chunk):
            hs = range(c * heads_per_chunk, (c + 1) * heads_per_chunk)
            lo, hi = c * V7X_MXU_WIDTH, (c + 1) * V7X_MXU_WIDTH
            s = jnp.dot(kc, qt[:, lo:hi], preferred_element_type=jnp.float32)
            s = s + _bias_columns(cstrip_ref, hs, cmp_offsets)
            m = jnp.max(s, axis=0, keepdims=True)
            e = jnp.exp2(s - m)
            l = jnp.maximum(jnp.sum(e, axis=0, keepdims=True), 1e-30)
            p = e * jnp.where(m > 0.5 * NEG_INF, 1.0 / l, 0.0)
            for k in range(heads_per_chunk):
                imp = imp + p[:, k * Q_TILE:(k + 1) * Q_TILE]
            p_ref[:n_rows, lo:hi] = p.astype(p_ref.dtype)
        gate_c = _gate_row(gt_ref, gate_base, heads)
        out_ref[...] = jnp.dot(vct_ref[0, :, :n_rows], p_ref[:n_rows, :], preferred_element_type=jnp.float32) * gate_c

        poolt = poolt_ref[:n_blocks, :n_rows]
        hi_t = imp.astype(jnp.bfloat16)
        r1 = imp - hi_t.astype(jnp.float32)
        mid_t = r1.astype(jnp.bfloat16)
        lo_t = (r1 - mid_t.astype(jnp.float32)).astype(jnp.bfloat16)
        imp_sel = (jnp.dot(poolt, hi_t, preferred_element_type=jnp.float32)
                   + jnp.dot(poolt, mid_t, preferred_element_type=jnp.float32)
                   + jnp.dot(poolt, lo_t, preferred_element_type=jnp.float32))

        j = lax.broadcasted_iota(jnp.int32, (n_blocks, Q_TILE), 0)
        t = t0 + lax.broadcasted_iota(jnp.int32, (n_blocks, Q_TILE), 1)
        cur = jnp.right_shift(t, SEL_BLOCK_LOG2)
        forced = jnp.logical_or(j == 0, jnp.logical_or(j == cur, j == cur - 1))
        valid = j * SEL_BLOCK <= t
        n_forced = 3
        score = jnp.where(jnp.logical_and(valid, jnp.logical_not(forced)), imp_sel, NEG_INF)
        picked = jnp.where(forced, 1.0, 0.0)
        jf = j.astype(jnp.float32)
        for _ in range(min(N_SELECT, n_sel) - n_forced):
            best = jnp.max(score, axis=0, keepdims=True)
            first = jnp.min(jnp.where(score == best, jf, float(n_blocks)), axis=0, keepdims=True)
            hit = jf == first
            picked = jnp.where(hit, 1.0, picked)
            score = jnp.where(hit, BELOW_NEG_INF, score)
        selm_ref[:n_blocks, :] = jnp.concatenate([picked - 1.0] * heads, axis=1)
        if n_blocks < n_sel:
            selm_ref[n_blocks:, :] = jnp.full((n_sel - n_blocks, lanes), -1.0, jnp.float32)

    blocks_step = n_sel // SELECT_VARIANTS
    variant = (t0 + Q_TILE - 1) // (SEL_BLOCK * blocks_step)
    for v in range(SELECT_VARIANTS):
        pl.when(variant == v)(functools.partial(compressed_and_select, (v + 1) * blocks_step))

    all_heads = range(heads)
    win_offsets = []
    for back in range(WIN_BLOCKS):
        for i0 in range(0, Q_TILE, V7X_SUBLANES):
            e = WIN_STRIP_CONST - back * Q_TILE + i0
            win_offsets.append(e if back == 0 else pl.multiple_of(jnp.where(back > qb, 0, e), V7X_SUBLANES))
    win_starts = [pl.multiple_of(jnp.maximum(qb - back, 0) * Q_TILE, Q_TILE) for back in range(WIN_BLOCKS)]
    k_w = jnp.concatenate([kw_ref[pl.ds(st, Q_TILE), :] for st in win_starts], axis=0)
    vt_w = jnp.concatenate([jnp.concatenate([vwt_ref[:, pl.ds(st, Q_TILE)] for st in win_starts], axis=1),
                            _ones_rows(WIN_BLOCKS * Q_TILE)], axis=0)
    s_w = jnp.dot(k_w, qt, preferred_element_type=jnp.float32) + _bias_columns(wstrip_ref, all_heads, win_offsets)
    p_w = jnp.exp2(s_w - jnp.max(s_w, axis=0, keepdims=True)).astype(jnp.bfloat16)
    acc_w = jnp.dot(vt_w, p_w, preferred_element_type=jnp.float32)
    gate_w = _gate_row(gt_ref, 2 * N_KV_GROUPS * heads + gate_base, heads)
    out_ref[...] += acc_w[:dh, :] * (gate_w / acc_w[dh:dh + 1, :])

    blocks_per_tile = KEY_TILE // SEL_BLOCK
    key_blk = jnp.right_shift(lax.broadcasted_iota(jnp.int32, (KEY_TILE, V7X_LANES), 0), SEL_BLOCK_LOG2)
    blk_col = lax.broadcasted_iota(jnp.int32, (KEY_TILE, V7X_LANES), 1)
    key_mask_cols = jnp.where(key_blk == blk_col, -NEG_INF, 0.0).astype(jnp.bfloat16)
    zero_rows = jnp.zeros((V7X_LANES - AUG_ROWS, lanes), jnp.bfloat16)
    sel_ones = _ones_rows(KEY_TILE)

    last_tile = (t0 + Q_TILE - 1) // KEY_TILE

    def scores(kt):
        start = pl.multiple_of(kt * KEY_TILE, KEY_TILE)
        k_aug = jnp.concatenate([ks_ref[pl.ds(start, KEY_TILE), :], key_mask_cols], axis=1)
        sel_rows = selm_ref[pl.ds(pl.multiple_of(kt * blocks_per_tile, blocks_per_tile), blocks_per_tile), :]
        sel_rows = jnp.concatenate([sel_rows, jnp.zeros((AUG_ROWS - blocks_per_tile, lanes), jnp.float32)], axis=0)
        qt_aug = jnp.concatenate([qt, sel_rows.astype(jnp.bfloat16), zero_rows], axis=0)
        row0 = SEL_STRIP_CONST - (t0 - kt * KEY_TILE)
        offsets = [pl.multiple_of(jnp.maximum(row0 + i0, 0), V7X_SUBLANES) for i0 in range(0, KEY_TILE, V7X_SUBLANES)]
        s = jnp.dot(k_aug, qt_aug, preferred_element_type=jnp.float32)
        return s + _bias_columns(sstrip_ref, all_heads, offsets)

    def values(kt):
        start = pl.multiple_of(kt * KEY_TILE, KEY_TILE)
        return jnp.concatenate([vst_ref[:, pl.ds(start, KEY_TILE)], sel_ones], axis=0)

    def sel_step(i, m):
        kt = 2 * i
        sb_ref[...] = scores(kt + 1)
        m = _online_softmax_tile(sa_ref[...], values(kt), m, acc_ref)
        sa_ref[...] = scores(kt + 2)
        return _online_softmax_tile(sb_ref[...], values(kt + 1), m, acc_ref)

    acc_ref[...] = jnp.zeros_like(acc_ref)
    sa_ref[...] = scores(0)
    m_run = lax.fori_loop(0, last_tile // 2, sel_step, jnp.full((1, lanes), NEG_INF, jnp.float32))
    tail = 2 * (last_tile // 2)

    @pl.when(tail < last_tile)
    def _():
        sb_ref[...] = scores(tail + 1)
        m = _online_softmax_tile(sa_ref[...], values(tail), m_run, acc_ref)
        _online_softmax_tile(sb_ref[...], values(tail + 1), m, acc_ref)

    @pl.when(tail == last_tile)
    def _():
        _online_softmax_tile(sa_ref[...], values(tail), m_run, acc_ref)

    gate_s = _gate_row(gt_ref, N_KV_GROUPS * heads + gate_base, heads)
    out_t = out_ref[...] + acc_ref[:dh, :] * (gate_s / acc_ref[dh:dh + 1, :])
    for h in range(heads):
        o_ref[:, h * dh:(h + 1) * dh] = out_t[:, h * Q_TILE:(h + 1) * Q_TILE].T.astype(o_ref.dtype)


def _pool_matrix_t(n_sel, n_cmp):
    r = SEL_BLOCK // CMP_STRIDE
    c = CMP_BLOCK // CMP_STRIDE
    j = jnp.arange(n_sel)[:, None]
    m = jnp.arange(n_cmp)[None, :]
    lo = r * j - (c - 1)
    return jnp.logical_and(m >= lo, m <= lo + r + c - 2).astype(jnp.bfloat16)


def nsa_attention(q_t, k_c, vt_c, k_sw, vt_sw, gates_t, strips):
    dq, s = q_t.shape
    dh = HEAD_DIM
    heads = dq // dh // N_KV_GROUPS
    assert heads == V7X_SUBLANES, "gate rows of one group must fill one sublane tile"
    assert KEY_TILE // SEL_BLOCK == V7X_SUBLANES, "one key tile's selection rows must fill one sublane tile"
    gw = heads * dh
    n_cmp = k_c.shape[1]
    n_sel = s // SEL_BLOCK
    sel_strip, win_strip, cmp_strip = strips
    once_per_group = pl.Buffered(1)

    def strip_spec(strip):
        return pl.BlockSpec((heads, strip.shape[1], V7X_LANES), lambda g, qb: (g, 0, 0), pipeline_mode=once_per_group)

    return pl.pallas_call(
        functools.partial(_nsa_body, heads=heads, n_cmp=n_cmp, n_sel=n_sel),
        grid=(N_KV_GROUPS, s // Q_TILE),
        in_specs=[
            pl.BlockSpec((gw, Q_TILE), lambda g, qb: (g, qb)),
            pl.BlockSpec((1, n_cmp, dh), lambda g, qb: (g, 0, 0)),
            pl.BlockSpec((1, dh, n_cmp), lambda g, qb: (g, 0, 0)),
            pl.BlockSpec((s, dh), lambda g, qb: (0, g), pipeline_mode=once_per_group),
            pl.BlockSpec((dh, s), lambda g, qb: (g, 0), pipeline_mode=once_per_group),
            pl.BlockSpec((s, dh), lambda g, qb: (0, N_KV_GROUPS + g), pipeline_mode=once_per_group),
            pl.BlockSpec((dh, s), lambda g, qb: (N_KV_GROUPS + g, 0), pipeline_mode=once_per_group),
            strip_spec(cmp_strip), strip_spec(sel_strip), strip_spec(win_strip),
            pl.BlockSpec((gates_t.shape[0], Q_TILE), lambda g, qb: (0, qb)),
            pl.BlockSpec((n_sel, n_cmp), lambda g, qb: (0, 0)),
        ],
        out_specs=pl.BlockSpec((Q_TILE, gw), lambda g, qb: (qb, g)),
        out_shape=jax.ShapeDtypeStruct((s, dq), jnp.bfloat16),
        scratch_shapes=[pltpu.VMEM((dh + AUG_ROWS, heads * Q_TILE), jnp.float32),
                        pltpu.VMEM((dh, heads * Q_TILE), jnp.float32),
                        pltpu.VMEM((n_cmp, heads * Q_TILE), jnp.bfloat16),
                        pltpu.VMEM((n_sel, heads * Q_TILE), jnp.float32),
                        pltpu.VMEM((KEY_TILE, heads * Q_TILE), jnp.float32),
                        pltpu.VMEM((KEY_TILE, heads * Q_TILE), jnp.float32)],
        compiler_params=_compiler_params(("arbitrary", "arbitrary")),
        name="nsa_attention",
    )(q_t, k_c, vt_c, k_sw, vt_sw, k_sw, vt_sw, cmp_strip, sel_strip, win_strip, gates_t,
      _pool_matrix_t(n_sel, n_cmp))


def _bf16(w):
    return w.astype(jnp.bfloat16)


def nsa_mixer(h, w_in, cmp_pos, cmp_w1, cmp_w2, w_out_all, layer, strips):
    dh, g = HEAD_DIM, N_KV_GROUPS
    q_end = w_out_all.shape[1]
    half = g * dh
    kv_end = q_end + N_BRANCHES * 2 * half
    n_gates = w_in.shape[1] - kv_end
    scale2 = dh ** -0.5 * LOG2_E

    def cols(branch, is_v):
        lo = q_end + branch * 2 * half + (half if is_v else 0)
        return w_in[:, lo:lo + half]

    q_t = matmul_nt(_bf16(w_in[:, :q_end].T), h, jnp.bfloat16, out_scale=scale2, name="nsa_q_proj")
    kv_cmp = matmul(h, _bf16(w_in[:, q_end:q_end + 2 * half]), jnp.float32, name="nsa_kv_cmp_proj")
    k_sw = matmul(h, _bf16(jnp.concatenate([cols(1, False), cols(2, False)], axis=1)), jnp.bfloat16,
                  name="nsa_k_proj")
    vt_sw = matmul_nt(_bf16(jnp.concatenate([cols(1, True), cols(2, True)], axis=1).T), h, jnp.bfloat16,
                      name="nsa_vt_proj")
    w_gate_t = jnp.pad(_bf16(w_in[:, kv_end:].T), ((0, V7X_LANES - n_gates), (0, 0)))
    gates_t = matmul_nt(w_gate_t, h, jnp.float32, sigmoid=True, name="nsa_gate_proj")

    kv_c = compress_kv(kv_cmp, cmp_pos, _bf16(cmp_w1), _bf16(cmp_w2))
    k_c = kv_c[:g]
    vt_c = jnp.swapaxes(kv_c[g:], 1, 2)
    o = nsa_attention(q_t, k_c, vt_c, k_sw, vt_sw, gates_t, strips)
    return matmul(o, w_out_all, jnp.float32, (layer,), name="nsa_out_proj")


def pool_mixer(h, w_in_all, w_group_all, scale, w_out_all, layer):
    y = pool_front(h, w_in_all, w_group_all, scale, layer)
    return matmul(y, w_out_all, jnp.float32, (layer,), name="pool_out_proj")


def kernel(x, norm_gains, ffn_w_gate, ffn_w_up, ffn_w_down, pool_w_in, pool_w_group, pool_scale, pool_w_out,
           nsa_w_in, nsa_cmp_pos, nsa_cmp_w1, nsa_cmp_w2, nsa_w_out, rel_bias):
    b, s, d = x.shape
    depth = norm_gains.shape[0]
    outs = []
    strips = nsa_strips(rel_bias) if depth > 1 else None
    wg, wu, wd = _bf16(ffn_w_gate), _bf16(ffn_w_up), _bf16(ffn_w_down)
    p_in, p_grp, p_out, n_out = _bf16(pool_w_in), _bf16(pool_w_group), _bf16(pool_w_out), _bf16(nsa_w_out)
    for bi in range(b):
        xs = x[bi]
        h = rms_norm_cast(xs, norm_gains[0, 0])
        for i in range(depth):
            gains = norm_gains[i]
            f1 = swiglu_ffn(h, wg, wu, wd, (i, 0))
            xs, h = residual_norm(xs, f1, gains[1], gains[2], 0.5)
            li = i // 2
            if i % 2 == 0:
                m = pool_mixer(h, p_in, p_grp, pool_scale[li], p_out, li)
            else:
                m = nsa_mixer(h, nsa_w_in[li], nsa_cmp_pos[li], nsa_cmp_w1[li], nsa_cmp_w2[li], n_out, li, strips)
            xs, h = residual_norm(xs, m, gains[3], gains[4], 1.0)
            f2 = swiglu_ffn(h, wg, wu, wd, (i, 1))
            g_next = norm_gains[i + 1, 0] if i + 1 < depth else None
            xs, h = residual_norm(xs, f2, gains[5], g_next, 0.5)
        outs.append(xs)
    return jnp.stack(outs, axis=0)
```
